```python
import math
import jax
import jax.numpy as jnp
from jax import lax
import numpy as np

D_MODEL = 4096
BATCH = 4
SEQ = 2048
DEPTH = 1

RET_HEAD_DIM = 256
RET_WIDTH = D_MODEL // 2
RET_HEADS = RET_WIDTH // RET_HEAD_DIM
RET_CHUNK = 128
ROPE_BASE = 10000.0
MOBA_HEAD_DIM = 128
MOBA_WIDTH = D_MODEL // 2
MOBA_HEADS = MOBA_WIDTH // MOBA_HEAD_DIM
MOBA_BLOCK = 256
MOBA_TOPK = 3
MOBA_Q_CHUNK = 16
REL_BUCKETS = 32
REL_MAX_DIST = 128
MIX_WIDTH = RET_WIDTH + MOBA_WIDTH
IN_COLS = 4 * RET_WIDTH + 3 * MOBA_WIDTH
SPLITS = [RET_WIDTH, 2 * RET_WIDTH, 3 * RET_WIDTH, 4 * RET_WIDTH,
          4 * RET_WIDTH + MOBA_WIDTH, 4 * RET_WIDTH + 2 * MOBA_WIDTH]
D_FF = 4 * D_MODEL
EPS = 1e-6
NEG_INF = -1e30

kernel_name = "hybrid_retention_moba_block"


def rmsnorm(x, g):
    x32 = x.astype(jnp.float32)
    r = x32 * lax.rsqrt(jnp.mean(x32 * x32, axis=-1, keepdims=True) + EPS)
    return (r * g.astype(jnp.float32)).astype(x.dtype)


def rotary(t, pos):
    half = t.shape[-1] // 2
    inv_freq = ROPE_BASE ** (-jnp.arange(half, dtype=jnp.float32) / half)
    ang = pos.astype(jnp.float32)[:, None] * inv_freq[None, :]
    cos = jnp.cos(ang).astype(t.dtype)
    sin = jnp.sin(ang).astype(t.dtype)
    t1, t2 = t[..., :half], t[..., half:]
    return jnp.concatenate([t1 * cos - t2 * sin, t1 * sin + t2 * cos], axis=-1)


def retention(q, k, v):
    B, H, S, d = q.shape
    pos = jnp.arange(S)
    q = rotary(q, pos)
    k = rotary(k, pos) * (d ** -0.5)
    log_gamma = jnp.log(1.0 - 2.0 ** (-5.0 - jnp.arange(H, dtype=jnp.float32)))
    C = RET_CHUNK
    N = S // C
    qc = q.reshape(B, H, N, C, d)
    kc = k.reshape(B, H, N, C, d)
    vc = v.reshape(B, H, N, C, d)
    idx = jnp.arange(C, dtype=jnp.float32)
    diff = idx[:, None] - idx[None, :]
    decay_mask = jnp.where(diff >= 0,
                           jnp.exp(log_gamma[:, None, None] * jnp.maximum(diff, 0.0)),
                           0.0).astype(q.dtype)
    inner = jnp.einsum('bhncd,bhnkd->bhnck', qc, kc) * decay_mask[None, :, None]
    y_inner = jnp.einsum('bhnck,bhnke->bhnce', inner, vc)
    key_decay = jnp.exp(log_gamma[:, None] * (C - 1 - idx)).astype(q.dtype)
    contrib = jnp.einsum('bhnkd,bhnke->bhnde', kc * key_decay[None, :, None, :, None], vc)
    chunk_decay = jnp.exp(log_gamma * C).astype(q.dtype)

    def step(state, c):
        return state * chunk_decay[None, :, None, None] + c, state

    init = jnp.zeros((B, H, d, d), dtype=contrib.dtype)
    _, prev = lax.scan(step, init, jnp.moveaxis(contrib, 2, 0))
    prev = jnp.moveaxis(prev, 0, 2)
    query_decay = jnp.exp(log_gamma[:, None] * (idx + 1.0)).astype(q.dtype)
    y_cross = jnp.einsum('bhncd,bhnde->bhnce', qc * query_decay[None, :, None, :, None], prev)
    return (y_inner + y_cross).reshape(B, H, S, d)


def t5_bucket(dist):
    n = jnp.maximum(dist, 0)
    max_exact = REL_BUCKETS // 2
    nf = jnp.maximum(n, 1).astype(jnp.float32)
    large = max_exact + (jnp.log(nf / max_exact) / math.log(REL_MAX_DIST / max_exact)
                         * (REL_BUCKETS - max_exact)).astype(jnp.int32)
    large = jnp.minimum(large, REL_BUCKETS - 1)
    return jnp.where(n < max_exact, n, large)


def moba_attention(q, k, v, rel_bias):
    B, H, S, d = q.shape
    L = MOBA_BLOCK
    NB = -(-S // L)
    K = min(MOBA_TOPK, NB)
    pad = NB * L - S
    kp = jnp.pad(k, ((0, 0), (0, 0), (0, pad), (0, 0)))
    vp = jnp.pad(v, ((0, 0), (0, 0), (0, pad), (0, 0)))
    kb = kp.reshape(B, H, NB, L, d)
    vb = vp.reshape(B, H, NB, L, d)
    k_mean = jnp.mean(kb.astype(jnp.float32), axis=3).astype(q.dtype)
    gate = jnp.einsum('bhsd,bhnd->bhsn', q, k_mean).astype(jnp.float32)
    q_block = jnp.arange(S) // L
    past = jnp.arange(NB)[None, :] < q_block[:, None]
    gate = jnp.where(past[None, None], gate, NEG_INF)
    _, sel = lax.top_k(gate, K)
    sel_valid = jnp.arange(K)[None, :] < q_block[:, None]

    scale = d ** -0.5
    Qc = MOBA_Q_CHUNK
    kidx = jnp.arange(L)
    bidx = jnp.arange(B)[:, None, None, None]
    hidx = jnp.arange(H)[None, :, None, None]
    bias_t = rel_bias.T

    def chunk(i):
        s0 = i * Qc
        qi = lax.dynamic_slice_in_dim(q, s0, Qc, axis=2)
        si = lax.dynamic_slice_in_dim(sel, s0, Qc, axis=2)
        vi = lax.dynamic_slice_in_dim(sel_valid, s0, Qc, axis=0)
        qpos = s0 + jnp.arange(Qc)
        blk = s0 // L
        k_sel = kb[bidx, hidx, si]
        v_sel = vb[bidx, hidx, si]
        logit_sel = jnp.einsum('bhqd,bhqkld->bhqkl', qi, k_sel).astype(jnp.float32) * scale
        dist_sel = qpos[None, None, :, None, None] - (si[..., None] * L + kidx)
        bias_sel = bias_t[hidx[..., None], t5_bucket(dist_sel)].astype(jnp.float32)
        logit_sel = jnp.where(vi[None, None, :, :, None], logit_sel + bias_sel, NEG_INF)
        k_own = lax.dynamic_slice_in_dim(kp, blk * L, L, axis=2)
        v_own = lax.dynamic_slice_in_dim(vp, blk * L, L, axis=2)
        logit_own = jnp.einsum('bhqd,bhld->bhql', qi, k_own).astype(jnp.float32) * scale
        dist_own = qpos[:, None] - (blk * L + kidx)[None, :]
        bias_own = jnp.transpose(rel_bias[t5_bucket(dist_own)], (2, 0, 1)).astype(jnp.float32)
        logit_own = jnp.where((dist_own >= 0)[None, None], logit_own + bias_own[None], NEG_INF)
        logits = jnp.concatenate([logit_sel.reshape(B, H, Qc, K * L), logit_own], axis=-1)
        p = jax.nn.softmax(logits, axis=-1).astype(v.dtype)
        out = (jnp.einsum('bhqm,bhqmd->bhqd', p[..., :K * L], v_sel.reshape(B, H, Qc, K * L, d))
               + jnp.einsum('bhql,bhld->bhqd', p[..., K * L:], v_own))
        return out

    out = lax.map(chunk, jnp.arange(S // Qc))
    return jnp.transpose(out, (1, 2, 0, 3, 4)).reshape(B, H, S, d)


def setup_inputs(seed: int = 0) -> dict:
    key = jax.random.key(seed)
    ks = jax.random.split(key, 10)
    f32 = jnp.float32
    x = jax.random.normal(ks[0], (BATCH, SEQ, D_MODEL), f32)
    norm_mix = 1.0 + 0.02 * jax.random.normal(ks[1], (DEPTH, D_MODEL), f32)
    w_in = jax.random.normal(ks[2], (DEPTH, D_MODEL, IN_COLS), f32) * D_MODEL ** -0.5
    rel_bias = 0.5 * jax.random.normal(ks[3], (REL_BUCKETS, MOBA_HEADS), f32)
    w_out = jax.random.normal(ks[4], (DEPTH, MIX_WIDTH, D_MODEL), f32) * MIX_WIDTH ** -0.5
    norm_ffn = 1.0 + 0.02 * jax.random.normal(ks[5], (DEPTH, D_MODEL), f32)
    w_up = jax.random.normal(ks[6], (DEPTH, D_MODEL, D_FF), f32) * D_MODEL ** -0.5
    w_down = jax.random.normal(ks[7], (DEPTH, D_FF, D_MODEL), f32) * D_FF ** -0.5
    norm_final = 1.0 + 0.02 * jax.random.normal(ks[8], (D_MODEL,), f32)
    return {"x": x, "norm_mix": norm_mix, "w_in": w_in, "rel_bias": rel_bias,
            "w_out": w_out, "norm_ffn": norm_ffn, "w_up": w_up, "w_down": w_down,
            "norm_final": norm_final}


def reference(x, norm_mix, w_in, rel_bias, w_out, norm_ffn, w_up, w_down, norm_final):
    B, S, _ = x.shape

    def heads(t, n_heads, hd):
        return t.reshape(B, S, n_heads, hd).transpose(0, 2, 1, 3)

    for l in range(DEPTH):
        h = rmsnorm(x, norm_mix[l])
        proj = h @ w_in[l]
        rq, rk, rv, rg, mq, mk, mv = jnp.split(proj, SPLITS, axis=-1)
        y_ret = retention(heads(rq, RET_HEADS, RET_HEAD_DIM),
                          heads(rk, RET_HEADS, RET_HEAD_DIM),
                          heads(rv, RET_HEADS, RET_HEAD_DIM))
        y32 = y_ret.astype(jnp.float32)
        y32 = y32 * lax.rsqrt(jnp.mean(y32 * y32, axis=-1, keepdims=True) + EPS)
        y_ret = y32.astype(x.dtype).transpose(0, 2, 1, 3).reshape(B, S, RET_WIDTH)
        y_ret = jax.nn.silu(rg) * y_ret
        y_moba = moba_attention(heads(mq, MOBA_HEADS, MOBA_HEAD_DIM),
                                heads(mk, MOBA_HEADS, MOBA_HEAD_DIM),
                                heads(mv, MOBA_HEADS, MOBA_HEAD_DIM), rel_bias)
        y_moba = y_moba.transpose(0, 2, 1, 3).reshape(B, S, MOBA_WIDTH)
        x = x + jnp.concatenate([y_ret, y_moba], axis=-1) @ w_out[l]
        h = rmsnorm(x, norm_ffn[l])
        x = x + jnp.square(jax.nn.relu(h @ w_up[l])) @ w_down[l]
    return rmsnorm(x, norm_final)
```

```python
import functools
import math

import numpy as np
import jax
import jax.numpy as jnp
from jax import lax
from jax.experimental import pallas as pl
from jax.experimental.pallas import tpu as pltpu

F32 = jnp.float32
BF16 = jnp.bfloat16

RET_HEAD_DIM = 256
RET_CHUNK = 128
ROPE_BASE = 10000.0
MOBA_HEAD_DIM = 128
MOBA_BLOCK = 256
MOBA_TOPK = 3
REL_BUCKETS = 32
REL_MAX_DIST = 128
EPS = 1e-6
NEG_INF = -1e30

V7X_VMEM_BYTES = 64 * 1024 * 1024
VMEM_LIMIT_BYTES = V7X_VMEM_BYTES - 8 * 1024 * 1024


def _params(semantics):
    return pltpu.CompilerParams(dimension_semantics=semantics,
                                vmem_limit_bytes=VMEM_LIMIT_BYTES)


def _rmsnorm_kernel(x_ref, g_ref, o_ref):
    x = x_ref[...]
    ms = jnp.mean(x * x, axis=-1, keepdims=True)
    o_ref[...] = (x * lax.rsqrt(ms + EPS) * g_ref[...]).astype(o_ref.dtype)


def _rmsnorm(x, g, out_dtype, rows=256):
    m, d = x.shape
    return pl.pallas_call(
        _rmsnorm_kernel,
        grid=(m // rows,),
        in_specs=[pl.BlockSpec((rows, d), lambda i: (i, 0)),
                  pl.BlockSpec((1, d), lambda i: (0, 0))],
        out_specs=pl.BlockSpec((rows, d), lambda i: (i, 0)),
        out_shape=jax.ShapeDtypeStruct((m, d), out_dtype),
        compiler_params=_params(("parallel",)),
        name="rmsnorm",
    )(x, g.reshape(1, d).astype(F32))


def _matmul_kernel(*refs, nk, act, has_res):
    if has_res:
        x_ref, w_ref, res_ref, o_ref = refs[:4]
        rest = refs[4:]
    else:
        x_ref, w_ref, o_ref = refs[:3]
        res_ref = None
        rest = refs[3:]

    def finish(acc):
        if act == "relu2":
            acc = jnp.square(jnp.maximum(acc, 0.0))
        if res_ref is not None:
            acc = res_ref[...] + acc
        o_ref[...] = acc.astype(o_ref.dtype)

    part = jnp.dot(x_ref[...], w_ref[...], preferred_element_type=F32)
    if nk == 1:
        finish(part)
        return
    acc_ref = rest[0]
    k = pl.program_id(2)

    @pl.when(k == 0)
    def _():
        acc_ref[...] = part

    @pl.when(jnp.logical_and(k > 0, k < nk - 1))
    def _():
        acc_ref[...] += part

    @pl.when(k == nk - 1)
    def _():
        finish(acc_ref[...] + part)


def _matmul(x, w, *, out_dtype, act=None, res=None, tm=1024, tn=1024, tk=None, name):
    m, kdim = x.shape
    _, n = w.shape
    tk = kdim if tk is None else tk
    nk = kdim // tk
    in_specs = [pl.BlockSpec((tm, tk), lambda i, j, k: (i, k)),
                pl.BlockSpec((tk, tn), lambda i, j, k: (k, j))]
    args = [x, w]
    if res is not None:
        in_specs.append(pl.BlockSpec((tm, tn), lambda i, j, k: (i, j)))
        args.append(res)
    scratch = [pltpu.VMEM((tm, tn), F32)] if nk > 1 else []
    return pl.pallas_call(
        functools.partial(_matmul_kernel, nk=nk, act=act, has_res=res is not None),
        grid=(m // tm, n // tn, nk),
        in_specs=in_specs,
        out_specs=pl.BlockSpec((tm, tn), lambda i, j, k: (i, j)),
        out_shape=jax.ShapeDtypeStruct((m, n), out_dtype),
        scratch_shapes=scratch,
        compiler_params=_params(("parallel", "parallel", "arbitrary")),
        name=name,
    )(*args)


def _retention_kernel(cdec_ref, q_ref, k_ref, v_ref, g_ref, cos_ref, sin_ref,
                      dmask_ref, qdec_ref, kdec_ref, o_ref, state_ref, *, seq):
    h = pl.program_id(1)
    c = RET_CHUNK
    half = RET_HEAD_DIM // 2
    cd = cdec_ref[h]
    k_scale = RET_HEAD_DIM ** -0.5
    state_ref[...] = jnp.zeros_like(state_ref)
    dmask = dmask_ref[...]
    qdec = qdec_ref[...]
    kdec = kdec_ref[...]

    def rot(t, cos, sin):
        t1, t2 = t[:, :half], t[:, half:]
        return jnp.concatenate([t1 * cos - t2 * sin, t1 * sin + t2 * cos], axis=-1)

    def chunk(n, carry):
        r0 = pl.multiple_of(n * c, c)
        rows = pl.ds(r0, c)
        cos = cos_ref[rows, :]
        sin = sin_ref[rows, :]
        qr = rot(q_ref[rows, :].astype(F32), cos, sin)
        kr = rot(k_ref[rows, :].astype(F32), cos, sin) * k_scale
        v = v_ref[rows, :]
        inner = lax.dot_general(qr.astype(BF16), kr.astype(BF16),
                                (((1,), (1,)), ((), ())),
                                preferred_element_type=F32) * dmask
        y = jnp.dot(inner.astype(BF16), v, preferred_element_type=F32)
        state = state_ref[...]
        y = y + jnp.dot((qr * qdec).astype(BF16), state.astype(BF16),
                        preferred_element_type=F32)
        contrib = lax.dot_general((kr * kdec).astype(BF16), v,
                                  (((0,), (0,)), ((), ())),
                                  preferred_element_type=F32)
        state_ref[...] = state * cd + contrib
        yn = y * lax.rsqrt(jnp.mean(y * y, axis=-1, keepdims=True) + EPS)
        g = g_ref[rows, :].astype(F32)
        gate = g * (1.0 / (1.0 + jnp.exp(-g)))
        o_ref[rows, :] = (gate * yn).astype(o_ref.dtype)
        return carry

    lax.fori_loop(0, seq // c, chunk, 0)


def _retention(proj, batch, seq, n_heads, col0):
    d = RET_HEAD_DIM
    c = RET_CHUNK
    half = d // 2
    width = n_heads * d
    blk0 = col0 // d

    pos = jnp.arange(seq)
    inv_freq = ROPE_BASE ** (-jnp.arange(half, dtype=F32) / half)
    ang = pos.astype(F32)[:, None] * inv_freq[None, :]
    cos = jnp.cos(ang)
    sin = jnp.sin(ang)
    log_gamma = jnp.log(1.0 - 2.0 ** (-5.0 - jnp.arange(n_heads, dtype=F32)))
    idx = jnp.arange(c, dtype=F32)
    diff = idx[:, None] - idx[None, :]
    dmask = jnp.where(diff >= 0,
                      jnp.exp(log_gamma[:, None, None] * jnp.maximum(diff, 0.0)),
                      0.0).astype(F32)
    kdec = jnp.exp(log_gamma[:, None] * (c - 1 - idx)).astype(F32)
    qdec = jnp.exp(log_gamma[:, None] * (idx + 1.0)).astype(F32)
    cdec = jnp.exp(log_gamma * c).astype(F32)
    kdec = jnp.broadcast_to(kdec[:, :, None], (n_heads, c, d))
    qdec = jnp.broadcast_to(qdec[:, :, None], (n_heads, c, d))

    def head_spec(group):
        return pl.BlockSpec((seq, d), lambda b, h: (b, blk0 + group * n_heads + h))

    return pl.pallas_call(
        functools.partial(_retention_kernel, seq=seq),
        grid=(batch, n_heads),
        in_specs=[pl.BlockSpec(memory_space=pltpu.SMEM),
                  head_spec(0), head_spec(1), head_spec(2), head_spec(3),
                  pl.BlockSpec((seq, half), lambda b, h: (0, 0)),
                  pl.BlockSpec((seq, half), lambda b, h: (0, 0)),
                  pl.BlockSpec((None, c, c), lambda b, h: (h, 0, 0)),
                  pl.BlockSpec((None, c, d), lambda b, h: (h, 0, 0)),
                  pl.BlockSpec((None, c, d), lambda b, h: (h, 0, 0))],
        out_specs=pl.BlockSpec((seq, d), lambda b, h: (b, h)),
        out_shape=jax.ShapeDtypeStruct((batch * seq, width), BF16),
        scratch_shapes=[pltpu.VMEM((d, d), F32)],
        compiler_params=_params(("parallel", "parallel")),
        name="retention",
    )(cdec, proj, proj, proj, proj, cos, sin, dmask, qdec, kdec)


def _t5_bucket(dist):
    n = jnp.maximum(dist, 0)
    max_exact = REL_BUCKETS // 2
    nf = jnp.maximum(n, 1).astype(F32)
    large = max_exact + (jnp.log(nf / max_exact) / math.log(REL_MAX_DIST / max_exact)
                         * (REL_BUCKETS - max_exact)).astype(jnp.int32)
    large = jnp.minimum(large, REL_BUCKETS - 1)
    return jnp.where(n < max_exact, n, large)


def _t5_bucket_static(n):
    n = np.maximum(n, 0)
    max_exact = REL_BUCKETS // 2
    nf = np.maximum(n, 1).astype(np.float32)
    large = max_exact + (np.log(nf / max_exact) / math.log(REL_MAX_DIST / max_exact)
                         * (REL_BUCKETS - max_exact)).astype(np.int32)
    large = np.minimum(large, REL_BUCKETS - 1)
    return np.where(n < max_exact, n, large)


def _moba_kernel(cfar_ref, q_ref, k_ref, v_ref, wrev_ref, o_ref,
                 t0_ref, t1_ref, sel_ref, *, seq):
    h = pl.program_id(0)
    b = pl.program_id(1)
    blk = MOBA_BLOCK
    d = MOBA_HEAD_DIM
    nb = seq // blk
    scale = d ** -0.5
    nt = (((1,), (1,)), ((), ()))

    @pl.when(b == 0)
    def _():
        w2 = jnp.broadcast_to(wrev_ref[...], (blk, 2 * blk))
        t0_ref[...] = pltpu.roll(w2, 0, 1, stride=1, stride_axis=0)[:, :blk]
        t1_ref[...] = pltpu.roll(w2, blk, 1, stride=1, stride_axis=0)[:, :blk]

    kf = k_ref[...].astype(F32)
    kmean = jnp.concatenate(
        [jnp.mean(kf[j * blk:(j + 1) * blk, :], axis=0, keepdims=True) for j in range(nb)],
        axis=0)
    gate = lax.dot_general(q_ref[...], kmean.astype(BF16), nt,
                           preferred_element_type=F32)
    qblk = lax.broadcasted_iota(jnp.int32, (seq, nb), 0) // blk
    jidx = lax.broadcasted_iota(jnp.int32, (seq, nb), 1)
    rank = jnp.zeros((seq, nb), jnp.int32)
    for jp in range(nb):
        col = gate[:, jp:jp + 1]
        beats = jnp.logical_or(col > gate,
                               jnp.logical_and(col == gate, jp < jidx))
        rank = rank + jnp.where(jnp.logical_and(beats, jp < qblk), 1, 0)
    selected = jnp.where(jnp.logical_and(jidx < qblk, rank < MOBA_TOPK), 1.0, 0.0)
    for j in range(nb):
        sel_ref[j] = selected[:, j:j + 1]

    cfar = cfar_ref[h]
    row = lax.broadcasted_iota(jnp.int32, (blk, blk), 0)
    colid = lax.broadcasted_iota(jnp.int32, (blk, blk), 1)
    causal = row >= colid

    for qb in range(nb):
        r0 = qb * blk
        q = q_ref[r0:r0 + blk, :]

        def step(j, carry, bias, mask):
            m, l, acc = carry
            k0 = j * blk if isinstance(j, int) else pl.multiple_of(j * blk, blk)
            kk = k_ref[pl.ds(k0, blk), :]
            vv = v_ref[pl.ds(k0, blk), :]
            s = lax.dot_general(q, kk, nt, preferred_element_type=F32) * scale + bias
            if mask is None:
                mask = sel_ref[j, pl.ds(r0, blk), :] > 0.5
            s = jnp.where(mask, s, NEG_INF)
            m_new = jnp.maximum(m, jnp.max(s, axis=-1, keepdims=True))
            alpha = jnp.exp(m - m_new)
            p = jnp.exp(s - m_new)
            l = alpha * l + jnp.sum(p, axis=-1, keepdims=True)
            acc = alpha * acc + jnp.dot(p.astype(BF16), vv, preferred_element_type=F32)
            return m_new, l, acc

        carry = (jnp.full((blk, 1), NEG_INF, F32), jnp.zeros((blk, 1), F32),
                 jnp.zeros((blk, d), F32))
        if qb >= 2:
            carry = lax.fori_loop(0, qb - 1,
                                  lambda j, cr: step(j, cr, cfar, None), carry)
        if qb >= 1:
            carry = step(qb - 1, carry, t1_ref[...], None)
        m, l, acc = step(qb, carry, t0_ref[...], causal)
        o_ref[r0:r0 + blk, :] = (acc / l).astype(o_ref.dtype)


def _moba(proj, rel_bias, batch, seq, n_heads, col0):
    d = MOBA_HEAD_DIM
    blk = MOBA_BLOCK
    nb = seq // blk
    width = n_heads * d
    blk0 = col0 // d

    far = _t5_bucket_static(np.arange(blk + 1, max(seq, blk + 2)))
    assert (far == far[0]).all(), "far-block bias is not constant"
    bvec = rel_bias[_t5_bucket(jnp.arange(2 * blk))].T.astype(F32)
    cfar = rel_bias[_t5_bucket(jnp.arange(blk + 1, blk + 2))[0]].astype(F32)
    wrev = jnp.roll(bvec[:, ::-1], 1, axis=1).reshape(n_heads, 1, 2 * blk)

    def head_spec(group):
        return pl.BlockSpec((seq, d), lambda h, b: (b, blk0 + group * n_heads + h))

    return pl.pallas_call(
        functools.partial(_moba_kernel, seq=seq),
        grid=(n_heads, batch),
        in_specs=[pl.BlockSpec(memory_space=pltpu.SMEM),
                  head_spec(0), head_spec(1), head_spec(2),
                  pl.BlockSpec((None, 1, 2 * blk), lambda h, b: (h, 0, 0))],
        out_specs=pl.BlockSpec((seq, d), lambda h, b: (b, h)),
        out_shape=jax.ShapeDtypeStruct((batch * seq, width), BF16),
        scratch_shapes=[pltpu.VMEM((blk, blk), F32), pltpu.VMEM((blk, blk), F32),
                        pltpu.VMEM((nb, seq, 1), F32)],
        compiler_params=_params(("arbitrary", "arbitrary")),
        name="moba",
    )(cfar, proj, proj, proj, wrev)


def kernel(x, norm_mix, w_in, rel_bias, w_out, norm_ffn, w_up, w_down, norm_final):
    batch, seq, d_model = x.shape
    depth = w_in.shape[0]
    n_moba = rel_bias.shape[1]
    moba_width = n_moba * MOBA_HEAD_DIM
    ret_width = w_out.shape[1] - moba_width
    n_ret = ret_width // RET_HEAD_DIM
    assert w_in.shape[2] == 4 * ret_width + 3 * moba_width
    assert seq % MOBA_BLOCK == 0 and seq % RET_CHUNK == 0

    xs = x.reshape(batch * seq, d_model)
    for l in range(depth):
        h = _rmsnorm(xs, norm_mix[l], BF16)
        proj = _matmul(h, w_in[l].astype(BF16), out_dtype=BF16, name="in_proj")
        y_ret = _retention(proj, batch, seq, n_ret, 0)
        y_moba = _moba(proj, rel_bias, batch, seq, n_moba, 4 * ret_width)
        mix = jnp.concatenate([y_ret, y_moba], axis=-1)
        xs = _matmul(mix, w_out[l].astype(BF16), out_dtype=F32, res=xs, name="out_proj")
        h = _rmsnorm(xs, norm_ffn[l], BF16)
        a = _matmul(h, w_up[l].astype(BF16), out_dtype=BF16, act="relu2", name="ffn_up")
        xs = _matmul(a, w_down[l].astype(BF16), out_dtype=F32, res=xs, tk=2048,
                     name="ffn_down")
    out = _rmsnorm(xs, norm_final, F32)
    return out.reshape(batch, seq, d_model)
```

```python
import functools
import math

import numpy as np
import jax
import jax.numpy as jnp
from jax import lax
from jax.experimental import pallas as pl
from jax.experimental.pallas import tpu as pltpu

F32 = jnp.float32
BF16 = jnp.bfloat16

RET_HEAD_DIM = 256
RET_CHUNK = 128
ROPE_BASE = 10000.0
MOBA_HEAD_DIM = 128
MOBA_BLOCK = 256
MOBA_TOPK = 3
REL_BUCKETS = 32
REL_MAX_DIST = 128
EPS = 1e-6
NEG_INF = -1e30
LOG2E = math.log2(math.e)

V7X_VMEM_BYTES = 64 * 1024 * 1024
VMEM_LIMIT_BYTES = V7X_VMEM_BYTES - 8 * 1024 * 1024
LANES = 128

NT_DIMS = (((1,), (1,)), ((), ()))
TN_DIMS = (((0,), (0,)), ((), ()))


def _params(semantics):
    return pltpu.CompilerParams(dimension_semantics=semantics,
                                vmem_limit_bytes=VMEM_LIMIT_BYTES)


def _rmsnorm_kernel(x_ref, g_ref, o_ref):
    x = x_ref[...]
    ms = jnp.mean(x * x, axis=-1, keepdims=True)
    o_ref[...] = (x * lax.rsqrt(ms + EPS) * g_ref[...]).astype(o_ref.dtype)


def _rmsnorm(x, g, out_dtype, rows=256):
    m, d = x.shape
    return pl.pallas_call(
        _rmsnorm_kernel,
        grid=(m // rows,),
        in_specs=[pl.BlockSpec((rows, d), lambda i: (i, 0)),
                  pl.BlockSpec((1, d), lambda i: (0, 0))],
        out_specs=pl.BlockSpec((rows, d), lambda i: (i, 0)),
        out_shape=jax.ShapeDtypeStruct((m, d), out_dtype),
        compiler_params=_params(("parallel",)),
        name="rmsnorm",
    )(x, g.reshape(1, d).astype(F32))


def _matmul_kernel(*refs, nk, act, has_res):
    if has_res:
        x_ref, w_ref, res_ref, o_ref = refs[:4]
        rest = refs[4:]
    else:
        x_ref, w_ref, o_ref = refs[:3]
        res_ref = None
        rest = refs[3:]

    def finish(acc):
        if act == "relu2":
            acc = jnp.square(jnp.maximum(acc, 0.0))
        if res_ref is not None:
            acc = res_ref[...] + acc
        o_ref[...] = acc.astype(o_ref.dtype)

    if nk == 1:
        finish(jnp.dot(x_ref[...], w_ref[...], preferred_element_type=F32))
        return
    acc_ref = rest[0]
    k = pl.program_id(2)

    @pl.when(k == 0)
    def _():
        acc_ref[...] = jnp.zeros_like(acc_ref)

    acc_ref[...] += jnp.dot(x_ref[...], w_ref[...], preferred_element_type=F32)

    @pl.when(k == nk - 1)
    def _():
        finish(acc_ref[...])


def _matmul(x, w, *, out_dtype, act=None, res=None, tm=1024, tn=1024, tk=None, name):
    m, kdim = x.shape
    _, n = w.shape
    tk = kdim if tk is None else tk
    nk = kdim // tk
    in_specs = [pl.BlockSpec((tm, tk), lambda i, j, k: (i, k)),
                pl.BlockSpec((tk, tn), lambda i, j, k: (k, j))]
    args = [x, w]
    if res is not None:
        in_specs.append(pl.BlockSpec((tm, tn), lambda i, j, k: (i, j)))
        args.append(res)
    scratch = [pltpu.VMEM((tm, tn), F32)] if nk > 1 else []
    return pl.pallas_call(
        functools.partial(_matmul_kernel, nk=nk, act=act, has_res=res is not None),
        grid=(m // tm, n // tn, nk),
        in_specs=in_specs,
        out_specs=pl.BlockSpec((tm, tn), lambda i, j, k: (i, j)),
        out_shape=jax.ShapeDtypeStruct((m, n), out_dtype),
        scratch_shapes=scratch,
        compiler_params=_params(("parallel", "parallel", "arbitrary")),
        name=name,
    )(*args)


def _retention_kernel(cdec_ref, q_ref, k_ref, v_ref, g_ref, cos_ref, sin_ref,
                      dmask_ref, qdec_ref, kdec_ref, o_ref, state_ref, *, seq):
    h = pl.program_id(1)
    c = RET_CHUNK
    half = RET_HEAD_DIM // 2
    cd = cdec_ref[h]
    k_scale = RET_HEAD_DIM ** -0.5
    state_ref[...] = jnp.zeros_like(state_ref)
    dmask = dmask_ref[...]
    qdec = qdec_ref[...]
    kdec = kdec_ref[...]

    def rot(t, cos, sin):
        t1, t2 = t[:, :half], t[:, half:]
        return jnp.concatenate([t1 * cos - t2 * sin, t1 * sin + t2 * cos], axis=-1)

    def chunk(n, carry):
        r0 = pl.multiple_of(n * c, c)
        rows = pl.ds(r0, c)
        cos = cos_ref[rows, :]
        sin = sin_ref[rows, :]
        qr = rot(q_ref[rows, :].astype(F32), cos, sin)
        kr = rot(k_ref[rows, :].astype(F32), cos, sin) * k_scale
        v = v_ref[rows, :]
        inner = lax.dot_general(qr.astype(BF16), kr.astype(BF16), NT_DIMS,
                                preferred_element_type=F32) * dmask
        y = jnp.dot(inner.astype(BF16), v, preferred_element_type=F32)
        state = state_ref[...]
        y = y + jnp.dot((qr * qdec).astype(BF16), state.astype(BF16),
                        preferred_element_type=F32)
        contrib = lax.dot_general((kr * kdec).astype(BF16), v, TN_DIMS,
                                  preferred_element_type=F32)
        state_ref[...] = state * cd + contrib
        yn = y * lax.rsqrt(jnp.mean(y * y, axis=-1, keepdims=True) + EPS)
        g = g_ref[rows, :].astype(F32)
        gate = g * (1.0 / (1.0 + jnp.exp(-g)))
        o_ref[rows, :] = (gate * yn).astype(o_ref.dtype)
        return carry

    lax.fori_loop(0, seq // c, chunk, 0)


def _retention(proj, batch, seq, n_heads, col0, out_width):
    d = RET_HEAD_DIM
    c = RET_CHUNK
    half = d // 2
    blk0 = col0 // d

    pos = jnp.arange(seq)
    inv_freq = ROPE_BASE ** (-jnp.arange(half, dtype=F32) / half)
    ang = pos.astype(F32)[:, None] * inv_freq[None, :]
    cos = jnp.cos(ang)
    sin = jnp.sin(ang)
    log_gamma = jnp.log(1.0 - 2.0 ** (-5.0 - jnp.arange(n_heads, dtype=F32)))
    idx = jnp.arange(c, dtype=F32)
    diff = idx[:, None] - idx[None, :]
    dmask = jnp.where(diff >= 0,
                      jnp.exp(log_gamma[:, None, None] * jnp.maximum(diff, 0.0)),
                      0.0).astype(F32)
    kdec = jnp.exp(log_gamma[:, None] * (c - 1 - idx)).astype(F32)
    qdec = jnp.exp(log_gamma[:, None] * (idx + 1.0)).astype(F32)
    cdec = jnp.exp(log_gamma * c).astype(F32)
    kdec = jnp.broadcast_to(kdec[:, :, None], (n_heads, c, d))
    qdec = jnp.broadcast_to(qdec[:, :, None], (n_heads, c, d))

    def head_spec(group):
        return pl.BlockSpec((seq, d), lambda b, h: (b, blk0 + group * n_heads + h))

    return pl.pallas_call(
        functools.partial(_retention_kernel, seq=seq),
        grid=(batch, n_heads),
        in_specs=[pl.BlockSpec(memory_space=pltpu.SMEM),
                  head_spec(0), head_spec(1), head_spec(2), head_spec(3),
                  pl.BlockSpec((seq, half), lambda b, h: (0, 0)),
                  pl.BlockSpec((seq, half), lambda b, h: (0, 0)),
                  pl.BlockSpec((None, c, c), lambda b, h: (h, 0, 0)),
                  pl.BlockSpec((None, c, d), lambda b, h: (h, 0, 0)),
                  pl.BlockSpec((None, c, d), lambda b, h: (h, 0, 0))],
        out_specs=pl.BlockSpec((seq, d), lambda b, h: (b, h)),
        out_shape=jax.ShapeDtypeStruct((batch * seq, out_width), BF16),
        scratch_shapes=[pltpu.VMEM((d, d), F32)],
        compiler_params=_params(("parallel", "parallel")),
        name="retention",
    )(cdec, proj, proj, proj, proj, cos, sin, dmask, qdec, kdec)


def _t5_bucket(dist):
    n = jnp.maximum(dist, 0)
    max_exact = REL_BUCKETS // 2
    nf = jnp.maximum(n, 1).astype(F32)
    large = max_exact + (jnp.log(nf / max_exact) / math.log(REL_MAX_DIST / max_exact)
                         * (REL_BUCKETS - max_exact)).astype(jnp.int32)
    large = jnp.minimum(large, REL_BUCKETS - 1)
    return jnp.where(n < max_exact, n, large)


def _t5_bucket_static(n):
    n = np.maximum(n, 0)
    max_exact = REL_BUCKETS // 2
    nf = np.maximum(n, 1).astype(np.float32)
    large = max_exact + (np.log(nf / max_exact) / math.log(REL_MAX_DIST / max_exact)
                         * (REL_BUCKETS - max_exact)).astype(np.int32)
    large = np.minimum(large, REL_BUCKETS - 1)
    return np.where(n < max_exact, n, large)


def _moba_kernel(cfar_ref, q_ref, k_ref, v_ref, wrev_ref, mix_ref, o_ref,
                 t0_ref, t1_ref, qa_ref, ka_ref, *, seq):
    del mix_ref
    h = pl.program_id(0)
    b = pl.program_id(1)
    blk = MOBA_BLOCK
    d = MOBA_HEAD_DIM
    nb = seq // blk
    cfar = cfar_ref[h]

    @pl.when(b == 0)
    def _():
        w2 = jnp.broadcast_to(wrev_ref[...], (blk, 2 * blk))
        t0 = pltpu.roll(w2, 0, 1, stride=1, stride_axis=0)[:, :blk]
        t1 = pltpu.roll(w2, blk, 1, stride=1, stride_axis=0)[:, :blk]
        row = lax.broadcasted_iota(jnp.int32, (blk, blk), 0)
        col = lax.broadcasted_iota(jnp.int32, (blk, blk), 1)
        t0_ref[...] = jnp.where(row >= col, (t0 - cfar) * LOG2E, NEG_INF)
        t1_ref[...] = (t1 - cfar) * LOG2E

    kf = k_ref[...].astype(F32)
    kmean = jnp.concatenate(
        [jnp.mean(kf[j * blk:(j + 1) * blk, :], axis=0, keepdims=True) for j in range(nb)],
        axis=0)
    gate = lax.dot_general(kmean.astype(BF16), q_ref[...], NT_DIMS,
                           preferred_element_type=F32)
    jidx = lax.broadcasted_iota(jnp.int32, (nb, seq), 0)
    qblk = lax.broadcasted_iota(jnp.int32, (nb, seq), 1) // blk
    rank = jnp.zeros((nb, seq), jnp.int32)
    for jp in range(nb):
        rowg = gate[jp:jp + 1, :]
        beats = jnp.logical_or(rowg > gate,
                               jnp.logical_and(rowg == gate, jp < jidx))
        rank = rank + jnp.where(jnp.logical_and(beats, jp < qblk), 1, 0)
    keep = jnp.logical_or(jidx >= qblk, rank < MOBA_TOPK)
    negmask = jnp.where(keep, 0.0, NEG_INF).astype(BF16)
    eye = (lax.broadcasted_iota(jnp.int32, (nb, LANES), 0) ==
           lax.broadcasted_iota(jnp.int32, (nb, LANES), 1)).astype(BF16)
    mask_cols = lax.dot_general(negmask, eye, TN_DIMS,
                                preferred_element_type=F32)

    qa_ref[:, :d] = (q_ref[...].astype(F32) * (d ** -0.5 * LOG2E)).astype(BF16)
    qa_ref[:, d:] = mask_cols.astype(BF16)
    ka_ref[:, :d] = k_ref[...]
    ka_ref[:, d:] = (lax.broadcasted_iota(jnp.int32, (seq, LANES), 0) // blk ==
                     lax.broadcasted_iota(jnp.int32, (seq, LANES), 1)).astype(BF16)

    t0 = t0_ref[...]
    t1 = t1_ref[...]
    for qb in range(nb):
        r0 = qb * blk
        w = r0 + blk
        s = lax.dot_general(qa_ref[r0:w, :], ka_ref[0:w, :], NT_DIMS,
                            preferred_element_type=F32)
        parts = [s[:, w - blk:] + t0]
        if qb >= 1:
            parts.insert(0, s[:, w - 2 * blk:w - blk] + t1)
        if qb >= 2:
            parts.insert(0, s[:, :w - 2 * blk])
        s = jnp.concatenate(parts, axis=1) if len(parts) > 1 else parts[0]
        m = jnp.max(s, axis=1, keepdims=True)
        p = jnp.exp2(s - m)
        l = jnp.sum(p, axis=1, keepdims=True)
        o = jnp.dot(p.astype(BF16), v_ref[0:w, :], preferred_element_type=F32)
        o_ref[r0:w, :] = (o / l).astype(o_ref.dtype)


def _moba(proj, mix, rel_bias, batch, seq, n_heads, col0, out_col0):
    d = MOBA_HEAD_DIM
    blk = MOBA_BLOCK
    blk0 = col0 // d
    out_blk0 = out_col0 // d
    assert d == LANES and seq // blk <= LANES

    far = _t5_bucket_static(np.arange(blk + 1, max(seq, blk + 2)))
    assert (far == far[0]).all(), "far-block bias is not constant"
    bvec = rel_bias[_t5_bucket(jnp.arange(2 * blk))].T.astype(F32)
    cfar = rel_bias[_t5_bucket(jnp.arange(blk + 1, blk + 2))[0]].astype(F32)
    wrev = jnp.roll(bvec[:, ::-1], 1, axis=1).reshape(n_heads, 1, 2 * blk)

    def head_spec(group):
        return pl.BlockSpec((seq, d), lambda h, b: (b, blk0 + group * n_heads + h))

    return pl.pallas_call(
        functools.partial(_moba_kernel, seq=seq),
        grid=(n_heads, batch),
        in_specs=[pl.BlockSpec(memory_space=pltpu.SMEM),
                  head_spec(0), head_spec(1), head_spec(2),
                  pl.BlockSpec((None, 1, 2 * blk), lambda h, b: (h, 0, 0)),
                  pl.BlockSpec(memory_space=pl.ANY)],
        out_specs=pl.BlockSpec((seq, d), lambda h, b: (b, out_blk0 + h)),
        out_shape=jax.ShapeDtypeStruct(mix.shape, mix.dtype),
        input_output_aliases={5: 0},
        scratch_shapes=[pltpu.VMEM((blk, blk), F32), pltpu.VMEM((blk, blk), F32),
                        pltpu.VMEM((seq, d + LANES), BF16),
                        pltpu.VMEM((seq, d + LANES), BF16)],
        compiler_params=_params(("arbitrary", "arbitrary")),
        name="moba",
    )(cfar, proj, proj, proj, wrev, mix)


def kernel(x, norm_mix, w_in, rel_bias, w_out, norm_ffn, w_up, w_down, norm_final):
    batch, seq, d_model = x.shape
    depth = w_in.shape[0]
    n_moba = rel_bias.shape[1]
    moba_width = n_moba * MOBA_HEAD_DIM
    mix_width = w_out.shape[1]
    ret_width = mix_width - moba_width
    n_ret = ret_width // RET_HEAD_DIM
    assert w_in.shape[2] == 4 * ret_width + 3 * moba_width
    assert seq % MOBA_BLOCK == 0 and seq % RET_CHUNK == 0

    xs = x.reshape(batch * seq, d_model)
    for l in range(depth):
        h = _rmsnorm(xs, norm_mix[l], BF16)
        proj = _matmul(h, w_in[l].astype(BF16), out_dtype=BF16, name="in_proj")
        mix = _retention(proj, batch, seq, n_ret, 0, mix_width)
        mix = _moba(proj, mix, rel_bias, batch, seq, n_moba, 4 * ret_width, ret_width)
        xs = _matmul(mix, w_out[l].astype(BF16), out_dtype=F32, res=xs, name="out_proj")
        h = _rmsnorm(xs, norm_ffn[l], BF16)
        a = _matmul(h, w_up[l].astype(BF16), out_dtype=BF16, act="relu2", name="ffn_up")
        xs = _matmul(a, w_down[l].astype(BF16), out_dtype=F32, res=xs, tk=2048,
                     name="ffn_down")
    out = _rmsnorm(xs, norm_final, F32)
    return out.reshape(batch, seq, d_model)
```

```python
import functools
import math

import numpy as np
import jax
import jax.numpy as jnp
from jax import lax
from jax.experimental import pallas as pl
from jax.experimental.pallas import tpu as pltpu

F32 = jnp.float32
BF16 = jnp.bfloat16

RET_HEAD_DIM = 256
RET_CHUNK = 128
ROPE_BASE = 10000.0
MOBA_HEAD_DIM = 128
MOBA_BLOCK = 256
MOBA_TOPK = 3
REL_BUCKETS = 32
REL_MAX_DIST = 128
EPS = 1e-6
NEG_INF = -1e30
LOG2E = math.log2(math.e)

V7X_VMEM_BYTES = 64 * 1024 * 1024
VMEM_LIMIT_BYTES = V7X_VMEM_BYTES - 8 * 1024 * 1024
LANES = 128

FULL_K_TILES = dict(tm=2048, tn=256)
FULL_K_RES_TILES = dict(tm=1024, tn=512)
SPLIT_K_TILES = dict(tm=2048, tn=1024, tk=1024)

NT_DIMS = (((1,), (1,)), ((), ()))
TN_DIMS = (((0,), (0,)), ((), ()))


def _params(semantics):
    return pltpu.CompilerParams(dimension_semantics=semantics,
                                vmem_limit_bytes=VMEM_LIMIT_BYTES)


def _rmsnorm_kernel(x_ref, g_ref, o_ref):
    x = x_ref[...]
    ms = jnp.mean(x * x, axis=-1, keepdims=True)
    o_ref[...] = (x * lax.rsqrt(ms + EPS) * g_ref[...]).astype(o_ref.dtype)


def _rmsnorm(x, g, out_dtype, rows=256):
    m, d = x.shape
    return pl.pallas_call(
        _rmsnorm_kernel,
        grid=(m // rows,),
        in_specs=[pl.BlockSpec((rows, d), lambda i: (i, 0)),
                  pl.BlockSpec((1, d), lambda i: (0, 0))],
        out_specs=pl.BlockSpec((rows, d), lambda i: (i, 0)),
        out_shape=jax.ShapeDtypeStruct((m, d), out_dtype),
        compiler_params=_params(("parallel",)),
        name="rmsnorm",
    )(x, g.reshape(1, d).astype(F32))


def _matmul_kernel(*refs, nk, act, has_res):
    if has_res:
        x_ref, w_ref, res_ref, o_ref = refs
    else:
        x_ref, w_ref, o_ref = refs
        res_ref = None

    def product():
        return jnp.dot(x_ref[...], w_ref[...].astype(BF16), preferred_element_type=F32)

    if nk == 1:
        acc = product()
        if act == "relu2":
            acc = jnp.square(jnp.maximum(acc, 0.0))
        if res_ref is not None:
            acc = res_ref[...] + acc
        o_ref[...] = acc.astype(o_ref.dtype)
        return

    k = pl.program_id(2)

    @pl.when(k == 0)
    def _():
        o_ref[...] = product() if res_ref is None else res_ref[...] + product()

    @pl.when(k > 0)
    def _():
        o_ref[...] += product()


def _matmul(x, w, *, out_dtype, act=None, res=None, tm, tn, tk=None, name):
    m, kdim = x.shape
    _, n = w.shape
    tk = kdim if tk is None else tk
    nk = kdim // tk
    assert m % tm == 0 and n % tn == 0 and kdim % tk == 0
    assert nk == 1 or (out_dtype == F32 and act is None)
    in_specs = [pl.BlockSpec((tm, tk), lambda i, j, k: (i, k)),
                pl.BlockSpec((tk, tn), lambda i, j, k: (k, j))]
    args = [x, w]
    if res is not None:
        in_specs.append(pl.BlockSpec((tm, tn), lambda i, j, k: (i, j)))
        args.append(res)
    return pl.pallas_call(
        functools.partial(_matmul_kernel, nk=nk, act=act, has_res=res is not None),
        grid=(m // tm, n // tn, nk),
        in_specs=in_specs,
        out_specs=pl.BlockSpec((tm, tn), lambda i, j, k: (i, j)),
        out_shape=jax.ShapeDtypeStruct((m, n), out_dtype),
        compiler_params=_params(("parallel", "parallel", "arbitrary")),
        name=name,
    )(*args)


def _retention_kernel(cdec_ref, q_ref, k_ref, v_ref, g_ref, cos_ref, sin_ref,
                      dmask_ref, qdec_ref, kdec_ref, o_ref, state_ref, *, seq):
    h = pl.program_id(1)
    c = RET_CHUNK
    half = RET_HEAD_DIM // 2
    cd = cdec_ref[h]
    k_scale = RET_HEAD_DIM ** -0.5
    state_ref[...] = jnp.zeros_like(state_ref)
    dmask = dmask_ref[...]
    qdec = qdec_ref[...]
    kdec = kdec_ref[...]

    def rot(t, cos, sin):
        t1, t2 = t[:, :half], t[:, half:]
        return jnp.concatenate([t1 * cos - t2 * sin, t1 * sin + t2 * cos], axis=-1)

    def chunk(n, carry):
        r0 = pl.multiple_of(n * c, c)
        rows = pl.ds(r0, c)
        cos = cos_ref[rows, :]
        sin = sin_ref[rows, :]
        qr = rot(q_ref[rows, :].astype(F32), cos, sin)
        kr = rot(k_ref[rows, :].astype(F32), cos, sin) * k_scale
        v = v_ref[rows, :]
        inner = lax.dot_general(qr.astype(BF16), kr.astype(BF16), NT_DIMS,
                                preferred_element_type=F32) * dmask
        y = jnp.dot(inner.astype(BF16), v, preferred_element_type=F32)
        state = state_ref[...]
        y = y + jnp.dot((qr * qdec).astype(BF16), state.astype(BF16),
                        preferred_element_type=F32)
        contrib = lax.dot_general((kr * kdec).astype(BF16), v, TN_DIMS,
                                  preferred_element_type=F32)
        state_ref[...] = state * cd + contrib
        yn = y * lax.rsqrt(jnp.mean(y * y, axis=-1, keepdims=True) + EPS)
        g = g_ref[rows, :].astype(F32)
        gate = g * (1.0 / (1.0 + jnp.exp(-g)))
        o_ref[rows, :] = (gate * yn).astype(o_ref.dtype)
        return carry

    lax.fori_loop(0, seq // c, chunk, 0)


def _retention(proj, batch, seq, n_heads, col0, out_width):
    d = RET_HEAD_DIM
    c = RET_CHUNK
    half = d // 2
    blk0 = col0 // d

    pos = jnp.arange(seq)
    inv_freq = ROPE_BASE ** (-jnp.arange(half, dtype=F32) / half)
    ang = pos.astype(F32)[:, None] * inv_freq[None, :]
    cos = jnp.cos(ang)
    sin = jnp.sin(ang)
    log_gamma = jnp.log(1.0 - 2.0 ** (-5.0 - jnp.arange(n_heads, dtype=F32)))
    idx = jnp.arange(c, dtype=F32)
    diff = idx[:, None] - idx[None, :]
    dmask = jnp.where(diff >= 0,
                      jnp.exp(log_gamma[:, None, None] * jnp.maximum(diff, 0.0)),
                      0.0).astype(F32)
    kdec = jnp.exp(log_gamma[:, None] * (c - 1 - idx)).astype(F32)
    qdec = jnp.exp(log_gamma[:, None] * (idx + 1.0)).astype(F32)
    cdec = jnp.exp(log_gamma * c).astype(F32)
    kdec = jnp.broadcast_to(kdec[:, :, None], (n_heads, c, d))
    qdec = jnp.broadcast_to(qdec[:, :, None], (n_heads, c, d))

    def head_spec(group):
        return pl.BlockSpec((seq, d), lambda b, h: (b, blk0 + group * n_heads + h))

    return pl.pallas_call(
        functools.partial(_retention_kernel, seq=seq),
        grid=(batch, n_heads),
        in_specs=[pl.BlockSpec(memory_space=pltpu.SMEM),
                  head_spec(0), head_spec(1), head_spec(2), head_spec(3),
                  pl.BlockSpec((seq, half), lambda b, h: (0, 0)),
                  pl.BlockSpec((seq, half), lambda b, h: (0, 0)),
                  pl.BlockSpec((None, c, c), lambda b, h: (h, 0, 0)),
                  pl.BlockSpec((None, c, d), lambda b, h: (h, 0, 0)),
                  pl.BlockSpec((None, c, d), lambda b, h: (h, 0, 0))],
        out_specs=pl.BlockSpec((seq, d), lambda b, h: (b, h)),
        out_shape=jax.ShapeDtypeStruct((batch * seq, out_width), BF16),
        scratch_shapes=[pltpu.VMEM((d, d), F32)],
        compiler_params=_params(("parallel", "parallel")),
        name="retention",
    )(cdec, proj, proj, proj, proj, cos, sin, dmask, qdec, kdec)


def _t5_bucket(dist):
    n = jnp.maximum(dist, 0)
    max_exact = REL_BUCKETS // 2
    nf = jnp.maximum(n, 1).astype(F32)
    large = max_exact + (jnp.log(nf / max_exact) / math.log(REL_MAX_DIST / max_exact)
                         * (REL_BUCKETS - max_exact)).astype(jnp.int32)
    large = jnp.minimum(large, REL_BUCKETS - 1)
    return jnp.where(n < max_exact, n, large)


def _t5_bucket_static(n):
    n = np.maximum(n, 0)
    max_exact = REL_BUCKETS // 2
    nf = np.maximum(n, 1).astype(np.float32)
    large = max_exact + (np.log(nf / max_exact) / math.log(REL_MAX_DIST / max_exact)
                         * (REL_BUCKETS - max_exact)).astype(np.int32)
    large = np.minimum(large, REL_BUCKETS - 1)
    return np.where(n < max_exact, n, large)


def _moba_kernel(cfar_ref, q_ref, k_ref, v_ref, wrev_ref, mix_ref, o_ref,
                 t0_ref, t1_ref, qa_ref, ka_ref, *, seq):
    del mix_ref
    h = pl.program_id(0)
    b = pl.program_id(1)
    blk = MOBA_BLOCK
    d = MOBA_HEAD_DIM
    nb = seq // blk
    cfar = cfar_ref[h]

    @pl.when(b == 0)
    def _():
        w2 = jnp.broadcast_to(wrev_ref[...], (blk, 2 * blk))
        t0 = pltpu.roll(w2, 0, 1, stride=1, stride_axis=0)[:, :blk]
        t1 = pltpu.roll(w2, blk, 1, stride=1, stride_axis=0)[:, :blk]
        row = lax.broadcasted_iota(jnp.int32, (blk, blk), 0)
        col = lax.broadcasted_iota(jnp.int32, (blk, blk), 1)
        t0_ref[...] = jnp.where(row >= col, (t0 - cfar) * LOG2E, NEG_INF)
        t1_ref[...] = (t1 - cfar) * LOG2E

    kf = k_ref[...].astype(F32)
    kmean = jnp.concatenate(
        [jnp.mean(kf[j * blk:(j + 1) * blk, :], axis=0, keepdims=True) for j in range(nb)],
        axis=0)
    gate = lax.dot_general(kmean.astype(BF16), q_ref[...], NT_DIMS,
                           preferred_element_type=F32)
    jidx = lax.broadcasted_iota(jnp.int32, (nb, seq), 0)
    qblk = lax.broadcasted_iota(jnp.int32, (nb, seq), 1) // blk
    rank = jnp.zeros((nb, seq), jnp.int32)
    for jp in range(nb):
        rowg = gate[jp:jp + 1, :]
        beats = jnp.logical_or(rowg > gate,
                               jnp.logical_and(rowg == gate, jp < jidx))
        rank = rank + jnp.where(jnp.logical_and(beats, jp < qblk), 1, 0)
    keep = jnp.logical_or(jidx >= qblk, rank < MOBA_TOPK)
    negmask = jnp.where(keep, 0.0, NEG_INF).astype(BF16)
    eye = (lax.broadcasted_iota(jnp.int32, (nb, LANES), 0) ==
           lax.broadcasted_iota(jnp.int32, (nb, LANES), 1)).astype(BF16)
    mask_cols = lax.dot_general(negmask, eye, TN_DIMS,
                                preferred_element_type=F32)

    qa_ref[:, :d] = (q_ref[...].astype(F32) * (d ** -0.5 * LOG2E)).astype(BF16)
    qa_ref[:, d:] = mask_cols.astype(BF16)
    ka_ref[:, :d] = k_ref[...]
    ka_ref[:, d:] = (lax.broadcasted_iota(jnp.int32, (seq, LANES), 0) // blk ==
                     lax.broadcasted_iota(jnp.int32, (seq, LANES), 1)).astype(BF16)

    t0 = t0_ref[...]
    t1 = t1_ref[...]
    for qb in range(nb):
        r0 = qb * blk
        w = r0 + blk
        s = lax.dot_general(qa_ref[r0:w, :], ka_ref[0:w, :], NT_DIMS,
                            preferred_element_type=F32)
        parts = [s[:, w - blk:] + t0]
        if qb >= 1:
            parts.insert(0, s[:, w - 2 * blk:w - blk] + t1)
        if qb >= 2:
            parts.insert(0, s[:, :w - 2 * blk])
        s = jnp.concatenate(parts, axis=1) if len(parts) > 1 else parts[0]
        m = jnp.max(s, axis=1, keepdims=True)
        p = jnp.exp2(s - m)
        l = jnp.sum(p, axis=1, keepdims=True)
        o = jnp.dot(p.astype(BF16), v_ref[0:w, :], preferred_element_type=F32)
        o_ref[r0:w, :] = (o / l).astype(o_ref.dtype)


def _moba(proj, mix, rel_bias, batch, seq, n_heads, col0, out_col0):
    d = MOBA_HEAD_DIM
    blk = MOBA_BLOCK
    blk0 = col0 // d
    out_blk0 = out_col0 // d
    assert d == LANES and seq // blk <= LANES

    far = _t5_bucket_static(np.arange(blk + 1, max(seq, blk + 2)))
    assert (far == far[0]).all(), "far-block bias is not constant"
    bvec = rel_bias[_t5_bucket(jnp.arange(2 * blk))].T.astype(F32)
    cfar = rel_bias[_t5_bucket(jnp.arange(blk + 1, blk + 2))[0]].astype(F32)
    wrev = jnp.roll(bvec[:, ::-1], 1, axis=1).reshape(n_heads, 1, 2 * blk)

    def head_spec(group):
        return pl.BlockSpec((seq, d), lambda h, b: (b, blk0 + group * n_heads + h))

    return pl.pallas_call(
        functools.partial(_moba_kernel, seq=seq),
        grid=(n_heads, batch),
        in_specs=[pl.BlockSpec(memory_space=pltpu.SMEM),
                  head_spec(0), head_spec(1), head_spec(2),
                  pl.BlockSpec((None, 1, 2 * blk), lambda h, b: (h, 0, 0)),
                  pl.BlockSpec(memory_space=pl.ANY)],
        out_specs=pl.BlockSpec((seq, d), lambda h, b: (b, out_blk0 + h)),
        out_shape=jax.ShapeDtypeStruct(mix.shape, mix.dtype),
        input_output_aliases={5: 0},
        scratch_shapes=[pltpu.VMEM((blk, blk), F32), pltpu.VMEM((blk, blk), F32),
                        pltpu.VMEM((seq, d + LANES), BF16),
                        pltpu.VMEM((seq, d + LANES), BF16)],
        compiler_params=_params(("arbitrary", "arbitrary")),
        name="moba",
    )(cfar, proj, proj, proj, wrev, mix)


def kernel(x, norm_mix, w_in, rel_bias, w_out, norm_ffn, w_up, w_down, norm_final):
    batch, seq, d_model = x.shape
    depth = w_in.shape[0]
    n_moba = rel_bias.shape[1]
    moba_width = n_moba * MOBA_HEAD_DIM
    mix_width = w_out.shape[1]
    ret_width = mix_width - moba_width
    n_ret = ret_width // RET_HEAD_DIM
    assert w_in.shape[2] == 4 * ret_width + 3 * moba_width
    assert seq % MOBA_BLOCK == 0 and seq % RET_CHUNK == 0

    xs = x.reshape(batch * seq, d_model)
    for l in range(depth):
        h = _rmsnorm(xs, norm_mix[l], BF16)
        proj = _matmul(h, w_in[l], out_dtype=BF16, name="in_proj", **FULL_K_TILES)
        mix = _retention(proj, batch, seq, n_ret, 0, mix_width)
        mix = _moba(proj, mix, rel_bias, batch, seq, n_moba, 4 * ret_width, ret_width)
        xs = _matmul(mix, w_out[l], out_dtype=F32, res=xs, name="out_proj",
                     **FULL_K_RES_TILES)
        h = _rmsnorm(xs, norm_ffn[l], BF16)
        a = _matmul(h, w_up[l], out_dtype=BF16, act="relu2", name="ffn_up", **FULL_K_TILES)
        xs = _matmul(a, w_down[l], out_dtype=F32, res=xs, name="ffn_down", **SPLIT_K_TILES)
    out = _rmsnorm(xs, norm_final, F32)
    return out.reshape(batch, seq, d_model)
```

```python
import functools
import math

import numpy as np
import jax
import jax.numpy as jnp
from jax import lax
from jax.experimental import pallas as pl
from jax.experimental.pallas import tpu as pltpu

F32 = jnp.float32
BF16 = jnp.bfloat16

RET_HEAD_DIM = 256
RET_CHUNK = 128
ROPE_BASE = 10000.0
MOBA_HEAD_DIM = 128
MOBA_BLOCK = 256
MOBA_TOPK = 3
REL_BUCKETS = 32
REL_MAX_DIST = 128
EPS = 1e-6
NEG_INF = -1e30
LOG2E = math.log2(math.e)

V7X_VMEM_BYTES = 64 * 1024 * 1024
VMEM_LIMIT_BYTES = V7X_VMEM_BYTES - 8 * 1024 * 1024
LANES = 128

FULL_K_TILES = dict(tm=2048, tn=512, lhs_buffers=1)
FULL_K_RES_TILES = dict(tm=1024, tn=512)
SPLIT_K_TILES = dict(tm=2048, tn=1024, tk=1024)

NT_DIMS = (((1,), (1,)), ((), ()))
TN_DIMS = (((0,), (0,)), ((), ()))


def _params(semantics):
    return pltpu.CompilerParams(dimension_semantics=semantics,
                                vmem_limit_bytes=VMEM_LIMIT_BYTES)


def _rmsnorm_kernel(x_ref, g_ref, o_ref):
    x = x_ref[...]
    ms = jnp.mean(x * x, axis=-1, keepdims=True)
    o_ref[...] = (x * lax.rsqrt(ms + EPS) * g_ref[...]).astype(o_ref.dtype)


def _rmsnorm(x, g, out_dtype, rows=256):
    m, d = x.shape
    return pl.pallas_call(
        _rmsnorm_kernel,
        grid=(m // rows,),
        in_specs=[pl.BlockSpec((rows, d), lambda i: (i, 0)),
                  pl.BlockSpec((1, d), lambda i: (0, 0))],
        out_specs=pl.BlockSpec((rows, d), lambda i: (i, 0)),
        out_shape=jax.ShapeDtypeStruct((m, d), out_dtype),
        compiler_params=_params(("parallel",)),
        name="rmsnorm",
    )(x, g.reshape(1, d).astype(F32))


def _matmul_kernel(*refs, n_lhs, nk, act, has_res):
    x_refs = refs[:n_lhs]
    w_ref = refs[n_lhs]
    res_ref = refs[n_lhs + 1] if has_res else None
    o_ref = refs[-1]

    def product():
        acc, k0 = None, 0
        for x_ref in x_refs:
            k1 = k0 + x_ref.shape[1]
            part = jnp.dot(x_ref[...], w_ref[k0:k1, :].astype(BF16),
                           preferred_element_type=F32)
            acc = part if acc is None else acc + part
            k0 = k1
        return acc

    if nk == 1:
        acc = product()
        if act == "relu2":
            acc = jnp.square(jnp.maximum(acc, 0.0))
        if res_ref is not None:
            acc = res_ref[...] + acc
        o_ref[...] = acc.astype(o_ref.dtype)
        return

    k = pl.program_id(2)

    @pl.when(k == 0)
    def _():
        o_ref[...] = product() if res_ref is None else res_ref[...] + product()

    @pl.when(k > 0)
    def _():
        o_ref[...] += product()


def _matmul(xs, w, *, out_dtype, act=None, res=None, tm, tn, tk=None,
            lhs_buffers=None, name):
    m = xs[0].shape[0]
    kdim, n = w.shape
    assert sum(x.shape[1] for x in xs) == kdim
    tk = kdim if tk is None else tk
    nk = kdim // tk
    assert m % tm == 0 and n % tn == 0 and kdim % tk == 0
    assert nk == 1 or (len(xs) == 1 and out_dtype == F32 and act is None)
    mode = {} if lhs_buffers is None else dict(pipeline_mode=pl.Buffered(lhs_buffers))
    if nk == 1:
        in_specs = [pl.BlockSpec((tm, x.shape[1]), lambda i, j, k: (i, 0), **mode)
                    for x in xs]
    else:
        in_specs = [pl.BlockSpec((tm, tk), lambda i, j, k: (i, k), **mode)]
    in_specs.append(pl.BlockSpec((tk, tn), lambda i, j, k: (k, j)))
    args = [*xs, w]
    if res is not None:
        in_specs.append(pl.BlockSpec((tm, tn), lambda i, j, k: (i, j)))
        args.append(res)
    return pl.pallas_call(
        functools.partial(_matmul_kernel, n_lhs=len(xs), nk=nk, act=act,
                          has_res=res is not None),
        grid=(m // tm, n // tn, nk),
        in_specs=in_specs,
        out_specs=pl.BlockSpec((tm, tn), lambda i, j, k: (i, j)),
        out_shape=jax.ShapeDtypeStruct((m, n), out_dtype),
        compiler_params=_params(("parallel", "parallel", "arbitrary")),
        name=name,
    )(*args)


def _retention_kernel(cdec_ref, q_ref, k_ref, v_ref, g_ref, cos_ref, sin_ref,
                      dmask_ref, qdec_ref, kdec_ref, o_ref, state_ref, *, seq):
    h = pl.program_id(1)
    c = RET_CHUNK
    half = RET_HEAD_DIM // 2
    cd = cdec_ref[h]
    k_scale = RET_HEAD_DIM ** -0.5
    state_ref[...] = jnp.zeros_like(state_ref)
    dmask = dmask_ref[...]
    qdec = qdec_ref[...]
    kdec = kdec_ref[...]

    def rot(t, cos, sin):
        t1, t2 = t[:, :half], t[:, half:]
        return jnp.concatenate([t1 * cos - t2 * sin, t1 * sin + t2 * cos], axis=-1)

    def chunk(n, carry):
        r0 = pl.multiple_of(n * c, c)
        rows = pl.ds(r0, c)
        cos = cos_ref[rows, :]
        sin = sin_ref[rows, :]
        qr = rot(q_ref[rows, :].astype(F32), cos, sin)
        kr = rot(k_ref[rows, :].astype(F32), cos, sin) * k_scale
        v = v_ref[rows, :]
        inner = lax.dot_general(qr.astype(BF16), kr.astype(BF16), NT_DIMS,
                                preferred_element_type=F32) * dmask
        y = jnp.dot(inner.astype(BF16), v, preferred_element_type=F32)
        state = state_ref[...]
        y = y + jnp.dot((qr * qdec).astype(BF16), state.astype(BF16),
                        preferred_element_type=F32)
        contrib = lax.dot_general((kr * kdec).astype(BF16), v, TN_DIMS,
                                  preferred_element_type=F32)
        state_ref[...] = state * cd + contrib
        yn = y * lax.rsqrt(jnp.mean(y * y, axis=-1, keepdims=True) + EPS)
        g = g_ref[rows, :].astype(F32)
        gate = g * (1.0 / (1.0 + jnp.exp(-g)))
        o_ref[rows, :] = (gate * yn).astype(o_ref.dtype)
        return carry

    lax.fori_loop(0, seq // c, chunk, 0, unroll=4)


def _retention(proj, batch, seq, n_heads, col0):
    d = RET_HEAD_DIM
    c = RET_CHUNK
    half = d // 2
    blk0 = col0 // d

    pos = jnp.arange(seq)
    inv_freq = ROPE_BASE ** (-jnp.arange(half, dtype=F32) / half)
    ang = pos.astype(F32)[:, None] * inv_freq[None, :]
    cos = jnp.cos(ang)
    sin = jnp.sin(ang)
    log_gamma = jnp.log(1.0 - 2.0 ** (-5.0 - jnp.arange(n_heads, dtype=F32)))
    idx = jnp.arange(c, dtype=F32)
    diff = idx[:, None] - idx[None, :]
    dmask = jnp.where(diff >= 0,
                      jnp.exp(log_gamma[:, None, None] * jnp.maximum(diff, 0.0)),
                      0.0).astype(F32)
    kdec = jnp.exp(log_gamma[:, None] * (c - 1 - idx)).astype(F32)
    qdec = jnp.exp(log_gamma[:, None] * (idx + 1.0)).astype(F32)
    cdec = jnp.exp(log_gamma * c).astype(F32)
    kdec = jnp.broadcast_to(kdec[:, :, None], (n_heads, c, d))
    qdec = jnp.broadcast_to(qdec[:, :, None], (n_heads, c, d))

    def head_spec(group):
        return pl.BlockSpec((seq, d), lambda b, h: (b, blk0 + group * n_heads + h))

    return pl.pallas_call(
        functools.partial(_retention_kernel, seq=seq),
        grid=(batch, n_heads),
        in_specs=[pl.BlockSpec(memory_space=pltpu.SMEM),
                  head_spec(0), head_spec(1), head_spec(2), head_spec(3),
                  pl.BlockSpec((seq, half), lambda b, h: (0, 0)),
                  pl.BlockSpec((seq, half), lambda b, h: (0, 0)),
                  pl.BlockSpec((None, c, c), lambda b, h: (h, 0, 0)),
                  pl.BlockSpec((None, c, d), lambda b, h: (h, 0, 0)),
                  pl.BlockSpec((None, c, d), lambda b, h: (h, 0, 0))],
        out_specs=pl.BlockSpec((seq, d), lambda b, h: (b, h)),
        out_shape=jax.ShapeDtypeStruct((batch * seq, n_heads * d), BF16),
        scratch_shapes=[pltpu.VMEM((d, d), F32)],
        compiler_params=_params(("parallel", "parallel")),
        name="retention",
    )(cdec, proj, proj, proj, proj, cos, sin, dmask, qdec, kdec)


def _t5_bucket(dist):
    n = jnp.maximum(dist, 0)
    max_exact = REL_BUCKETS // 2
    nf = jnp.maximum(n, 1).astype(F32)
    large = max_exact + (jnp.log(nf / max_exact) / math.log(REL_MAX_DIST / max_exact)
                         * (REL_BUCKETS - max_exact)).astype(jnp.int32)
    large = jnp.minimum(large, REL_BUCKETS - 1)
    return jnp.where(n < max_exact, n, large)


def _t5_bucket_static(n):
    n = np.maximum(n, 0)
    max_exact = REL_BUCKETS // 2
    nf = np.maximum(n, 1).astype(np.float32)
    large = max_exact + (np.log(nf / max_exact) / math.log(REL_MAX_DIST / max_exact)
                         * (REL_BUCKETS - max_exact)).astype(np.int32)
    large = np.minimum(large, REL_BUCKETS - 1)
    return np.where(n < max_exact, n, large)


def _moba_kernel(cfar_ref, q_ref, k_ref, v_ref, wrev_ref, o_ref,
                 t0_ref, t1_ref, qa_ref, ka_ref, *, seq):
    h = pl.program_id(0)
    b = pl.program_id(1)
    blk = MOBA_BLOCK
    d = MOBA_HEAD_DIM
    nb = seq // blk
    cfar = cfar_ref[h]

    @pl.when(b == 0)
    def _():
        w2 = jnp.broadcast_to(wrev_ref[...], (blk, 2 * blk))
        t0 = pltpu.roll(w2, 0, 1, stride=1, stride_axis=0)[:, :blk]
        t1 = pltpu.roll(w2, blk, 1, stride=1, stride_axis=0)[:, :blk]
        row = lax.broadcasted_iota(jnp.int32, (blk, blk), 0)
        col = lax.broadcasted_iota(jnp.int32, (blk, blk), 1)
        t0_ref[...] = jnp.where(row >= col, (t0 - cfar) * LOG2E, NEG_INF)
        t1_ref[...] = (t1 - cfar) * LOG2E

    kf = k_ref[...].astype(F32)
    kmean = jnp.concatenate(
        [jnp.mean(kf[j * blk:(j + 1) * blk, :], axis=0, keepdims=True) for j in range(nb)],
        axis=0)
    gate = lax.dot_general(kmean.astype(BF16), q_ref[...], NT_DIMS,
                           preferred_element_type=F32)
    jidx = lax.broadcasted_iota(jnp.int32, (nb, seq), 0)
    qblk = lax.broadcasted_iota(jnp.int32, (nb, seq), 1) // blk
    rank = jnp.zeros((nb, seq), jnp.int32)
    for jp in range(nb):
        rowg = gate[jp:jp + 1, :]
        beats = jnp.logical_or(rowg > gate,
                               jnp.logical_and(rowg == gate, jp < jidx))
        rank = rank + jnp.where(jnp.logical_and(beats, jp < qblk), 1, 0)
    keep = jnp.logical_or(jidx >= qblk, rank < MOBA_TOPK)
    negmask = jnp.where(keep, 0.0, NEG_INF).astype(BF16)
    eye = (lax.broadcasted_iota(jnp.int32, (nb, LANES), 0) ==
           lax.broadcasted_iota(jnp.int32, (nb, LANES), 1)).astype(BF16)
    mask_cols = lax.dot_general(negmask, eye, TN_DIMS,
                                preferred_element_type=F32)

    qa_ref[:, :d] = (q_ref[...].astype(F32) * (d ** -0.5 * LOG2E)).astype(BF16)
    qa_ref[:, d:] = mask_cols.astype(BF16)
    ka_ref[:, :d] = k_ref[...]
    ka_ref[:, d:] = (lax.broadcasted_iota(jnp.int32, (seq, LANES), 0) // blk ==
                     lax.broadcasted_iota(jnp.int32, (seq, LANES), 1)).astype(BF16)

    t0 = t0_ref[...]
    t1 = t1_ref[...]
    for qb in range(nb):
        r0 = qb * blk
        w = r0 + blk
        s = lax.dot_general(qa_ref[r0:w, :], ka_ref[0:w, :], NT_DIMS,
                            preferred_element_type=F32)
        parts = [s[:, w - blk:] + t0]
        if qb >= 1:
            parts.insert(0, s[:, w - 2 * blk:w - blk] + t1)
        if qb >= 2:
            parts.insert(0, s[:, :w - 2 * blk])
        s = jnp.concatenate(parts, axis=1) if len(parts) > 1 else parts[0]
        m = jnp.max(s, axis=1, keepdims=True)
        p = jnp.exp2(s - m)
        l = jnp.sum(p, axis=1, keepdims=True)
        o = jnp.dot(p.astype(BF16), v_ref[0:w, :], preferred_element_type=F32)
        o_ref[r0:w, :] = (o / l).astype(o_ref.dtype)


def _moba(proj, rel_bias, batch, seq, n_heads, col0):
    d = MOBA_HEAD_DIM
    blk = MOBA_BLOCK
    blk0 = col0 // d
    assert d == LANES and seq // blk <= LANES

    far = _t5_bucket_static(np.arange(blk + 1, max(seq, blk + 2)))
    assert (far == far[0]).all(), "far-block bias is not constant"
    bvec = rel_bias[_t5_bucket(jnp.arange(2 * blk))].T.astype(F32)
    cfar = rel_bias[_t5_bucket(jnp.arange(blk + 1, blk + 2))[0]].astype(F32)
    wrev = jnp.roll(bvec[:, ::-1], 1, axis=1).reshape(n_heads, 1, 2 * blk)

    def head_spec(group):
        return pl.BlockSpec((seq, d), lambda h, b: (b, blk0 + group * n_heads + h))

    return pl.pallas_call(
        functools.partial(_moba_kernel, seq=seq),
        grid=(n_heads, batch),
        in_specs=[pl.BlockSpec(memory_space=pltpu.SMEM),
                  head_spec(0), head_spec(1), head_spec(2),
                  pl.BlockSpec((None, 1, 2 * blk), lambda h, b: (h, 0, 0))],
        out_specs=pl.BlockSpec((seq, d), lambda h, b: (b, h)),
        out_shape=jax.ShapeDtypeStruct((batch * seq, n_heads * d), BF16),
        scratch_shapes=[pltpu.VMEM((blk, blk), F32), pltpu.VMEM((blk, blk), F32),
                        pltpu.VMEM((seq, d + LANES), BF16),
                        pltpu.VMEM((seq, d + LANES), BF16)],
        compiler_params=_params(("arbitrary", "arbitrary")),
        name="moba",
    )(cfar, proj, proj, proj, wrev)


def kernel(x, norm_mix, w_in, rel_bias, w_out, norm_ffn, w_up, w_down, norm_final):
    batch, seq, d_model = x.shape
    depth = w_in.shape[0]
    n_moba = rel_bias.shape[1]
    moba_width = n_moba * MOBA_HEAD_DIM
    mix_width = w_out.shape[1]
    ret_width = mix_width - moba_width
    n_ret = ret_width // RET_HEAD_DIM
    assert w_in.shape[2] == 4 * ret_width + 3 * moba_width
    assert seq % MOBA_BLOCK == 0 and seq % RET_CHUNK == 0

    xs = x.reshape(batch * seq, d_model)
    for l in range(depth):
        h = _rmsnorm(xs, norm_mix[l], BF16)
        proj = _matmul([h], w_in[l], out_dtype=BF16, name="in_proj", **FULL_K_TILES)
        y_ret = _retention(proj, batch, seq, n_ret, 0)
        y_moba = _moba(proj, rel_bias, batch, seq, n_moba, 4 * ret_width)
        xs = _matmul([y_ret, y_moba], w_out[l], out_dtype=F32, res=xs, name="out_proj",
                     **FULL_K_RES_TILES)
        h = _rmsnorm(xs, norm_ffn[l], BF16)
        a = _matmul([h], w_up[l], out_dtype=BF16, act="relu2", name="ffn_up",
                    **FULL_K_TILES)
        xs = _matmul([a], w_down[l], out_dtype=F32, res=xs, name="ffn_down",
                     **SPLIT_K_TILES)
    out = _rmsnorm(xs, norm_final, F32)
    return out.reshape(batch, seq, d_model)
```

```python
import functools
import math

import numpy as np
import jax
import jax.numpy as jnp
from jax import lax
from jax.experimental import pallas as pl
from jax.experimental.pallas import tpu as pltpu

F32 = jnp.float32
BF16 = jnp.bfloat16

RET_HEAD_DIM = 256
RET_CHUNK = 128
ROPE_BASE = 10000.0
MOBA_HEAD_DIM = 128
MOBA_BLOCK = 256
MOBA_TOPK = 3
REL_BUCKETS = 32
REL_MAX_DIST = 128
EPS = 1e-6
NEG_INF = -1e30
LOG2E = math.log2(math.e)

V7X_VMEM_BYTES = 64 * 1024 * 1024
VMEM_LIMIT_BYTES = V7X_VMEM_BYTES - 8 * 1024 * 1024
LANES = 128

FULL_K_TILES = dict(tm=2048, tn=512, lhs_buffers=1)
FULL_K_RES_TILES = dict(tm=1024, tn=512)
SPLIT_K_TILES = dict(tm=2048, tn=1024, tk=1024)

NT_DIMS = (((1,), (1,)), ((), ()))
TN_DIMS = (((0,), (0,)), ((), ()))


def _params(semantics):
    return pltpu.CompilerParams(dimension_semantics=semantics,
                                vmem_limit_bytes=VMEM_LIMIT_BYTES)


def _rmsnorm_kernel(x_ref, g_ref, o_ref):
    x = x_ref[...]
    ms = jnp.mean(x * x, axis=-1, keepdims=True)
    o_ref[...] = (x * lax.rsqrt(ms + EPS) * g_ref[...]).astype(o_ref.dtype)


def _rmsnorm(x, g, out_dtype, rows=256):
    m, d = x.shape
    return pl.pallas_call(
        _rmsnorm_kernel,
        grid=(m // rows,),
        in_specs=[pl.BlockSpec((rows, d), lambda i: (i, 0)),
                  pl.BlockSpec((1, d), lambda i: (0, 0))],
        out_specs=pl.BlockSpec((rows, d), lambda i: (i, 0)),
        out_shape=jax.ShapeDtypeStruct((m, d), out_dtype),
        compiler_params=_params(("parallel",)),
        name="rmsnorm",
    )(x, g.reshape(1, d).astype(F32))


def _matmul_kernel(*refs, n_lhs, nk, act, has_res):
    x_refs = refs[:n_lhs]
    w_ref = refs[n_lhs]
    res_ref = refs[n_lhs + 1] if has_res else None
    o_ref = refs[-1]

    def product():
        acc, k0 = None, 0
        for x_ref in x_refs:
            k1 = k0 + x_ref.shape[1]
            part = jnp.dot(x_ref[...], w_ref[k0:k1, :].astype(BF16),
                           preferred_element_type=F32)
            acc = part if acc is None else acc + part
            k0 = k1
        return acc

    if nk == 1:
        acc = product()
        if act == "relu2":
            acc = jnp.square(jnp.maximum(acc, 0.0))
        if res_ref is not None:
            acc = res_ref[...] + acc
        o_ref[...] = acc.astype(o_ref.dtype)
        return

    k = pl.program_id(2)

    @pl.when(k == 0)
    def _():
        o_ref[...] = product() if res_ref is None else res_ref[...] + product()

    @pl.when(k > 0)
    def _():
        o_ref[...] += product()


def _matmul(xs, w, *, out_dtype, act=None, res=None, tm, tn, tk=None,
            lhs_buffers=None, name):
    m = xs[0].shape[0]
    kdim, n = w.shape
    assert sum(x.shape[1] for x in xs) == kdim
    tk = kdim if tk is None else tk
    nk = kdim // tk
    assert m % tm == 0 and n % tn == 0 and kdim % tk == 0
    assert nk == 1 or (len(xs) == 1 and out_dtype == F32 and act is None)
    mode = {} if lhs_buffers is None else dict(pipeline_mode=pl.Buffered(lhs_buffers))
    if nk == 1:
        in_specs = [pl.BlockSpec((tm, x.shape[1]), lambda i, j, k: (i, 0), **mode)
                    for x in xs]
    else:
        in_specs = [pl.BlockSpec((tm, tk), lambda i, j, k: (i, k), **mode)]
    in_specs.append(pl.BlockSpec((tk, tn), lambda i, j, k: (k, j)))
    args = [*xs, w]
    if res is not None:
        in_specs.append(pl.BlockSpec((tm, tn), lambda i, j, k: (i, j)))
        args.append(res)
    return pl.pallas_call(
        functools.partial(_matmul_kernel, n_lhs=len(xs), nk=nk, act=act,
                          has_res=res is not None),
        grid=(m // tm, n // tn, nk),
        in_specs=in_specs,
        out_specs=pl.BlockSpec((tm, tn), lambda i, j, k: (i, j)),
        out_shape=jax.ShapeDtypeStruct((m, n), out_dtype),
        compiler_params=_params(("parallel", "parallel", "arbitrary")),
        name=name,
    )(*args)


def _retention_kernel(cdec_ref, q_ref, k_ref, v_ref, g_ref, cos_ref, sin_ref,
                      dmask_ref, qdec_ref, kdec_ref, o_ref, state_ref, *, seq):
    h = pl.program_id(1)
    c = RET_CHUNK
    half = RET_HEAD_DIM // 2
    cd = cdec_ref[h]
    k_scale = RET_HEAD_DIM ** -0.5
    state_ref[...] = jnp.zeros_like(state_ref)
    dmask = dmask_ref[...]
    qdec = qdec_ref[...]
    kdec = kdec_ref[...]

    def rot(t, cos, sin):
        t1, t2 = t[:, :half], t[:, half:]
        return jnp.concatenate([t1 * cos - t2 * sin, t1 * sin + t2 * cos], axis=-1)

    def chunk(n, carry):
        r0 = pl.multiple_of(n * c, c)
        rows = pl.ds(r0, c)
        cos = cos_ref[rows, :]
        sin = sin_ref[rows, :]
        qr = rot(q_ref[rows, :].astype(F32), cos, sin)
        kr = rot(k_ref[rows, :].astype(F32), cos, sin) * k_scale
        v = v_ref[rows, :]
        inner = lax.dot_general(qr.astype(BF16), kr.astype(BF16), NT_DIMS,
                                preferred_element_type=F32) * dmask
        y = jnp.dot(inner.astype(BF16), v, preferred_element_type=F32)
        state = state_ref[...]
        y = y + jnp.dot((qr * qdec).astype(BF16), state.astype(BF16),
                        preferred_element_type=F32)
        contrib = lax.dot_general((kr * kdec).astype(BF16), v, TN_DIMS,
                                  preferred_element_type=F32)
        state_ref[...] = state * cd + contrib
        yn = y * lax.rsqrt(jnp.mean(y * y, axis=-1, keepdims=True) + EPS)
        g = g_ref[rows, :].astype(F32)
        gate = g * (1.0 / (1.0 + jnp.exp(-g)))
        o_ref[rows, :] = (gate * yn).astype(o_ref.dtype)
        return carry

    lax.fori_loop(0, seq // c, chunk, 0, unroll=8)


def _retention(proj, batch, seq, n_heads, col0):
    d = RET_HEAD_DIM
    c = RET_CHUNK
    half = d // 2
    blk0 = col0 // d

    pos = jnp.arange(seq)
    inv_freq = ROPE_BASE ** (-jnp.arange(half, dtype=F32) / half)
    ang = pos.astype(F32)[:, None] * inv_freq[None, :]
    cos = jnp.cos(ang)
    sin = jnp.sin(ang)
    log_gamma = jnp.log(1.0 - 2.0 ** (-5.0 - jnp.arange(n_heads, dtype=F32)))
    idx = jnp.arange(c, dtype=F32)
    diff = idx[:, None] - idx[None, :]
    dmask = jnp.where(diff >= 0,
                      jnp.exp(log_gamma[:, None, None] * jnp.maximum(diff, 0.0)),
                      0.0).astype(F32)
    kdec = jnp.exp(log_gamma[:, None] * (c - 1 - idx)).astype(F32)
    qdec = jnp.exp(log_gamma[:, None] * (idx + 1.0)).astype(F32)
    cdec = jnp.exp(log_gamma * c).astype(F32)
    kdec = jnp.broadcast_to(kdec[:, :, None], (n_heads, c, d))
    qdec = jnp.broadcast_to(qdec[:, :, None], (n_heads, c, d))

    def head_spec(group):
        return pl.BlockSpec((seq, d), lambda b, h: (b, blk0 + group * n_heads + h))

    return pl.pallas_call(
        functools.partial(_retention_kernel, seq=seq),
        grid=(batch, n_heads),
        in_specs=[pl.BlockSpec(memory_space=pltpu.SMEM),
                  head_spec(0), head_spec(1), head_spec(2), head_spec(3),
                  pl.BlockSpec((seq, half), lambda b, h: (0, 0)),
                  pl.BlockSpec((seq, half), lambda b, h: (0, 0)),
                  pl.BlockSpec((None, c, c), lambda b, h: (h, 0, 0)),
                  pl.BlockSpec((None, c, d), lambda b, h: (h, 0, 0)),
                  pl.BlockSpec((None, c, d), lambda b, h: (h, 0, 0))],
        out_specs=pl.BlockSpec((seq, d), lambda b, h: (b, h)),
        out_shape=jax.ShapeDtypeStruct((batch * seq, n_heads * d), BF16),
        scratch_shapes=[pltpu.VMEM((d, d), F32)],
        compiler_params=_params(("parallel", "parallel")),
        name="retention",
    )(cdec, proj, proj, proj, proj, cos, sin, dmask, qdec, kdec)


def _t5_bucket(dist):
    n = jnp.maximum(dist, 0)
    max_exact = REL_BUCKETS // 2
    nf = jnp.maximum(n, 1).astype(F32)
    large = max_exact + (jnp.log(nf / max_exact) / math.log(REL_MAX_DIST / max_exact)
                         * (REL_BUCKETS - max_exact)).astype(jnp.int32)
    large = jnp.minimum(large, REL_BUCKETS - 1)
    return jnp.where(n < max_exact, n, large)


def _t5_bucket_static(n):
    n = np.maximum(n, 0)
    max_exact = REL_BUCKETS // 2
    nf = np.maximum(n, 1).astype(np.float32)
    large = max_exact + (np.log(nf / max_exact) / math.log(REL_MAX_DIST / max_exact)
                         * (REL_BUCKETS - max_exact)).astype(np.int32)
    large = np.minimum(large, REL_BUCKETS - 1)
    return np.where(n < max_exact, n, large)


def _reduce_rows(pair_op, final_op, x, rows=32):
    slabs = [x[i:i + rows, :] for i in range(0, x.shape[0], rows)]
    while len(slabs) > 1:
        nxt = [pair_op(slabs[i], slabs[i + 1]) for i in range(0, len(slabs) - 1, 2)]
        if len(slabs) % 2:
            nxt.append(slabs[-1])
        slabs = nxt
    return final_op(slabs[0], axis=0, keepdims=True)


def _moba_kernel(cfar_ref, q_ref, k_ref, v_ref, wrev_ref, o_ref,
                 t0_ref, t1_ref, qa_ref, ka_ref, vt_ref, s_ref, p_ref, *, seq):
    h = pl.program_id(0)
    b = pl.program_id(1)
    blk = MOBA_BLOCK
    d = MOBA_HEAD_DIM
    nb = seq // blk
    cfar = cfar_ref[h]

    @pl.when(b == 0)
    def _():
        w2 = jnp.broadcast_to(wrev_ref[...], (blk, 2 * blk))
        t0 = pltpu.roll(w2, 0, 1, stride=1, stride_axis=0)[:, :blk]
        t1 = pltpu.roll(w2, blk, 1, stride=1, stride_axis=0)[:, :blk]
        row = lax.broadcasted_iota(jnp.int32, (blk, blk), 0)
        col = lax.broadcasted_iota(jnp.int32, (blk, blk), 1)
        t0_ref[...] = jnp.where(row >= col, (t0 - cfar) * LOG2E, NEG_INF).T
        t1_ref[...] = ((t1 - cfar) * LOG2E).T

    kf = k_ref[...].astype(F32)
    kmean = jnp.concatenate(
        [jnp.mean(kf[j * blk:(j + 1) * blk, :], axis=0, keepdims=True) for j in range(nb)],
        axis=0)
    gate = lax.dot_general(kmean.astype(BF16), q_ref[...], NT_DIMS,
                           preferred_element_type=F32)
    jidx = lax.broadcasted_iota(jnp.int32, (nb, seq), 0)
    qblk = lax.broadcasted_iota(jnp.int32, (nb, seq), 1) // blk
    rank = jnp.zeros((nb, seq), jnp.int32)
    for jp in range(nb):
        rowg = gate[jp:jp + 1, :]
        beats = jnp.logical_or(rowg > gate,
                               jnp.logical_and(rowg == gate, jp < jidx))
        rank = rank + jnp.where(jnp.logical_and(beats, jp < qblk), 1, 0)
    keep = jnp.logical_or(jidx >= qblk, rank < MOBA_TOPK)
    negmask = jnp.where(keep, 0.0, NEG_INF).astype(BF16)
    eye = (lax.broadcasted_iota(jnp.int32, (nb, LANES), 0) ==
           lax.broadcasted_iota(jnp.int32, (nb, LANES), 1)).astype(BF16)
    mask_cols = lax.dot_general(negmask, eye, TN_DIMS,
                                preferred_element_type=F32)

    qa_ref[:, :d] = (q_ref[...].astype(F32) * (d ** -0.5 * LOG2E)).astype(BF16)
    qa_ref[:, d:] = mask_cols.astype(BF16)
    ka_ref[:, :d] = k_ref[...]
    ka_ref[:, d:] = (lax.broadcasted_iota(jnp.int32, (seq, LANES), 0) // blk ==
                     lax.broadcasted_iota(jnp.int32, (seq, LANES), 1)).astype(BF16)

    vt_ref[...] = v_ref[...].astype(F32).T.astype(BF16)

    t0 = t0_ref[...]
    t1 = t1_ref[...]
    def scores(qb):
        r0 = qb * blk
        w = r0 + blk
        s = lax.dot_general(ka_ref[0:w, :], qa_ref[r0:w, :], NT_DIMS,
                            preferred_element_type=F32)
        slot = qb % 2
        s_ref[slot, w - blk:w, :] = s[w - blk:, :] + t0
        if qb >= 1:
            s_ref[slot, w - 2 * blk:w - blk, :] = s[w - 2 * blk:w - blk, :] + t1
        if qb >= 2:
            s_ref[slot, 0:w - 2 * blk, :] = s[:w - 2 * blk, :]

    def softmax(qb):
        w = (qb + 1) * blk
        s = s_ref[qb % 2, 0:w, :]
        m = _reduce_rows(jnp.maximum, jnp.max, s)
        p = jnp.exp2(s - m)
        p_ref[qb % 2, 0:w, :] = p.astype(BF16)
        return _reduce_rows(jnp.add, jnp.sum, p)

    def values(qb, l):
        r0 = qb * blk
        w = r0 + blk
        o = jnp.dot(vt_ref[:, 0:w], p_ref[qb % 2, 0:w, :],
                    preferred_element_type=F32)
        o_ref[r0:w, :] = (o / l).T.astype(o_ref.dtype)

    scores(0)
    row_sum = softmax(0)
    if nb > 1:
        scores(1)
    for qb in range(nb):
        if qb + 2 < nb:
            scores(qb + 2)
        next_sum = softmax(qb + 1) if qb + 1 < nb else None
        values(qb, row_sum)
        row_sum = next_sum


def _moba(proj, rel_bias, batch, seq, n_heads, col0):
    d = MOBA_HEAD_DIM
    blk = MOBA_BLOCK
    blk0 = col0 // d
    assert d == LANES and seq // blk <= LANES

    far = _t5_bucket_static(np.arange(blk + 1, max(seq, blk + 2)))
    assert (far == far[0]).all(), "far-block bias is not constant"
    bvec = rel_bias[_t5_bucket(jnp.arange(2 * blk))].T.astype(F32)
    cfar = rel_bias[_t5_bucket(jnp.arange(blk + 1, blk + 2))[0]].astype(F32)
    wrev = jnp.roll(bvec[:, ::-1], 1, axis=1).reshape(n_heads, 1, 2 * blk)

    def head_spec(group):
        return pl.BlockSpec((seq, d), lambda h, b: (b, blk0 + group * n_heads + h))

    return pl.pallas_call(
        functools.partial(_moba_kernel, seq=seq),
        grid=(n_heads, batch),
        in_specs=[pl.BlockSpec(memory_space=pltpu.SMEM),
                  head_spec(0), head_spec(1), head_spec(2),
                  pl.BlockSpec((None, 1, 2 * blk), lambda h, b: (h, 0, 0))],
        out_specs=pl.BlockSpec((seq, d), lambda h, b: (b, h)),
        out_shape=jax.ShapeDtypeStruct((batch * seq, n_heads * d), BF16),
        scratch_shapes=[pltpu.VMEM((blk, blk), F32), pltpu.VMEM((blk, blk), F32),
                        pltpu.VMEM((seq, d + LANES), BF16),
                        pltpu.VMEM((seq, d + LANES), BF16),
                        pltpu.VMEM((d, seq), BF16),
                        pltpu.VMEM((2, seq, blk), F32),
                        pltpu.VMEM((2, seq, blk), BF16)],
        compiler_params=_params(("arbitrary", "arbitrary")),
        name="moba",
    )(cfar, proj, proj, proj, wrev)


def kernel(x, norm_mix, w_in, rel_bias, w_out, norm_ffn, w_up, w_down, norm_final):
    batch, seq, d_model = x.shape
    depth = w_in.shape[0]
    n_moba = rel_bias.shape[1]
    moba_width = n_moba * MOBA_HEAD_DIM
    mix_width = w_out.shape[1]
    ret_width = mix_width - moba_width
    n_ret = ret_width // RET_HEAD_DIM
    assert w_in.shape[2] == 4 * ret_width + 3 * moba_width
    assert seq % MOBA_BLOCK == 0 and seq % RET_CHUNK == 0

    xs = x.reshape(batch * seq, d_model)
    for l in range(depth):
        h = _rmsnorm(xs, norm_mix[l], BF16)
        proj = _matmul([h], w_in[l], out_dtype=BF16, name="in_proj", **FULL_K_TILES)
        y_ret = _retention(proj, batch, seq, n_ret, 0)
        y_moba = _moba(proj, rel_bias, batch, seq, n_moba, 4 * ret_width)
        xs = _matmul([y_ret, y_moba], w_out[l], out_dtype=F32, res=xs, name="out_proj",
                     **FULL_K_RES_TILES)
        h = _rmsnorm(xs, norm_ffn[l], BF16)
        a = _matmul([h], w_up[l], out_dtype=BF16, act="relu2", name="ffn_up",
                    **FULL_K_TILES)
        xs = _matmul([a], w_down[l], out_dtype=F32, res=xs, name="ffn_down",
                     **SPLIT_K_TILES)
    out = _rmsnorm(xs, norm_final, F32)
    return out.reshape(batch, seq, d_model)
```

```python
import functools
import math

import numpy as np
import jax
import jax.numpy as jnp
from jax import lax
from jax.experimental import pallas as pl
from jax.experimental.pallas import tpu as pltpu

F32 = jnp.float32
BF16 = jnp.bfloat16

RET_HEAD_DIM = 256
RET_CHUNK = 128
ROPE_BASE = 10000.0
MOBA_HEAD_DIM = 128
MOBA_BLOCK = 256
MOBA_TOPK = 3
REL_BUCKETS = 32
REL_MAX_DIST = 128
EPS = 1e-6
NEG_INF = -1e30
LOG2E = math.log2(math.e)

V7X_VMEM_BYTES = 64 * 1024 * 1024
VMEM_LIMIT_BYTES = V7X_VMEM_BYTES - 4 * 1024 * 1024
LANES = 128

FULL_K_TILES = dict(tm=2048, tn=512, lhs_buffers=1)
FULL_K_RES_TILES = dict(tm=1024, tn=512)
SPLIT_K_TILES = dict(tm=2048, tn=1024, tk=1024)

NT_DIMS = (((1,), (1,)), ((), ()))
TN_DIMS = (((0,), (0,)), ((), ()))


def _params(semantics):
    return pltpu.CompilerParams(dimension_semantics=semantics,
                                vmem_limit_bytes=VMEM_LIMIT_BYTES)


def _rmsnorm_kernel(x_ref, g_ref, o_ref):
    x = x_ref[...]
    ms = jnp.mean(x * x, axis=-1, keepdims=True)
    o_ref[...] = (x * lax.rsqrt(ms + EPS) * g_ref[...]).astype(o_ref.dtype)


def _rmsnorm(x, g, out_dtype, rows=256):
    m, d = x.shape
    return pl.pallas_call(
        _rmsnorm_kernel,
        grid=(m // rows,),
        in_specs=[pl.BlockSpec((rows, d), lambda i: (i, 0)),
                  pl.BlockSpec((1, d), lambda i: (0, 0))],
        out_specs=pl.BlockSpec((rows, d), lambda i: (i, 0)),
        out_shape=jax.ShapeDtypeStruct((m, d), out_dtype),
        compiler_params=_params(("parallel",)),
        name="rmsnorm",
    )(x, g.reshape(1, d).astype(F32))


def _matmul_kernel(*refs, n_lhs, nk, act, has_res, emit_prenorm, has_row_ssq, norm_dim):
    refs = list(refs)
    x_refs = [refs.pop(0) for _ in range(n_lhs)]
    w_ref = refs.pop(0)
    res_ref = refs.pop(0) if has_res else None
    gain_ref = refs.pop(0) if emit_prenorm else None
    ssq_in_ref = refs.pop(0) if has_row_ssq else None
    o_ref = refs.pop(0)
    if emit_prenorm:
        hg_ref, ssq_ref = refs

    def product():
        acc, k0 = None, 0
        for x_ref in x_refs:
            k1 = k0 + x_ref.shape[1]
            part = jnp.dot(x_ref[...], w_ref[k0:k1, :].astype(BF16),
                           preferred_element_type=F32)
            acc = part if acc is None else acc + part
            k0 = k1
        return acc

    def row_factor(width):
        r2 = 1.0 / (ssq_in_ref[...] / norm_dim + EPS)
        return jnp.concatenate([r2] * (width // LANES), axis=1)

    if nk == 1:
        acc = product()
        if act == "relu2":
            acc = jnp.square(jnp.maximum(acc, 0.0))
        if ssq_in_ref is not None:
            acc = acc * row_factor(acc.shape[1])
        if res_ref is not None:
            acc = res_ref[...] + acc
        o_ref[...] = acc.astype(o_ref.dtype)
        if emit_prenorm:
            hg_ref[...] = (acc * gain_ref[...]).astype(hg_ref.dtype)
            part = jnp.broadcast_to(jnp.sum(acc * acc, axis=1, keepdims=True),
                                    ssq_ref.shape)
            j = pl.program_id(1)

            @pl.when(j == 0)
            def _():
                ssq_ref[...] = part

            @pl.when(j > 0)
            def _():
                ssq_ref[...] += part
        return

    k = pl.program_id(2)

    @pl.when(k == 0)
    def _():
        o_ref[...] = product()

    @pl.when(jnp.logical_and(k > 0, k < nk - 1))
    def _():
        o_ref[...] += product()

    @pl.when(k == nk - 1)
    def _():
        acc = o_ref[...] + product()
        if ssq_in_ref is not None:
            acc = acc * row_factor(acc.shape[1])
        if res_ref is not None:
            acc = res_ref[...] + acc
        o_ref[...] = acc


def _matmul(xs, w, *, out_dtype, act=None, res=None, prenorm_gain=None, row_ssq=None,
            norm_dim=None, tm, tn, tk=None, lhs_buffers=None, name):
    m = xs[0].shape[0]
    kdim, n = w.shape
    assert sum(x.shape[1] for x in xs) == kdim
    tk = kdim if tk is None else tk
    nk = kdim // tk
    assert m % tm == 0 and n % tn == 0 and kdim % tk == 0
    assert nk == 1 or (len(xs) == 1 and out_dtype == F32 and act is None
                       and prenorm_gain is None)
    assert (row_ssq is None) == (norm_dim is None)
    mode = {} if lhs_buffers is None else dict(pipeline_mode=pl.Buffered(lhs_buffers))
    if nk == 1:
        in_specs = [pl.BlockSpec((tm, x.shape[1]), lambda i, j, k: (i, 0), **mode)
                    for x in xs]
    else:
        in_specs = [pl.BlockSpec((tm, tk), lambda i, j, k: (i, k), **mode)]
    in_specs.append(pl.BlockSpec((tk, tn), lambda i, j, k: (k, j)))
    args = [*xs, w]
    if res is not None:
        in_specs.append(pl.BlockSpec((tm, tn), lambda i, j, k: (i, j)))
        args.append(res)
    out_specs = pl.BlockSpec((tm, tn), lambda i, j, k: (i, j))
    out_shape = jax.ShapeDtypeStruct((m, n), out_dtype)
    row_spec = pl.BlockSpec((tm, LANES), lambda i, j, k: (i, 0))
    if prenorm_gain is not None:
        in_specs.append(pl.BlockSpec((1, tn), lambda i, j, k: (0, j)))
        args.append(prenorm_gain.reshape(1, n).astype(F32))
        out_specs = [out_specs, pl.BlockSpec((tm, tn), lambda i, j, k: (i, j)), row_spec]
        out_shape = [out_shape, jax.ShapeDtypeStruct((m, n), BF16),
                     jax.ShapeDtypeStruct((m, LANES), F32)]
    if row_ssq is not None:
        in_specs.append(row_spec)
        args.append(row_ssq)
    return pl.pallas_call(
        functools.partial(_matmul_kernel, n_lhs=len(xs), nk=nk, act=act,
                          has_res=res is not None,
                          emit_prenorm=prenorm_gain is not None,
                          has_row_ssq=row_ssq is not None, norm_dim=norm_dim),
        grid=(m // tm, n // tn, nk),
        in_specs=in_specs,
        out_specs=out_specs,
        out_shape=out_shape,
        compiler_params=_params(("parallel", "arbitrary", "arbitrary")),
        name=name,
    )(*args)


def _retention_kernel(cdec_ref, q_ref, k_ref, v_ref, g_ref, cos_ref, sin_ref,
                      dmask_ref, qdec_ref, kdec_ref, o_ref, state_ref, *, seq):
    h = pl.program_id(1)
    c = RET_CHUNK
    half = RET_HEAD_DIM // 2
    cd = cdec_ref[h]
    k_scale = RET_HEAD_DIM ** -0.5
    state_ref[...] = jnp.zeros_like(state_ref)
    dmask = dmask_ref[...]
    qdec = qdec_ref[...]
    kdec = kdec_ref[...]

    def rot(t, cos, sin):
        t1, t2 = t[:, :half], t[:, half:]
        return jnp.concatenate([t1 * cos - t2 * sin, t1 * sin + t2 * cos], axis=-1)

    def chunk(n, carry):
        r0 = pl.multiple_of(n * c, c)
        rows = pl.ds(r0, c)
        cos = cos_ref[rows, :]
        sin = sin_ref[rows, :]
        qr = rot(q_ref[rows, :].astype(F32), cos, sin)
        kr = rot(k_ref[rows, :].astype(F32), cos, sin) * k_scale
        v = v_ref[rows, :]
        inner = lax.dot_general(qr.astype(BF16), kr.astype(BF16), NT_DIMS,
                                preferred_element_type=F32) * dmask
        y = jnp.dot(inner.astype(BF16), v, preferred_element_type=F32)
        state = state_ref[...]
        y = y + jnp.dot((qr * qdec).astype(BF16), state.astype(BF16),
                        preferred_element_type=F32)
        contrib = lax.dot_general((kr * kdec).astype(BF16), v, TN_DIMS,
                                  preferred_element_type=F32)
        state_ref[...] = state * cd + contrib
        yn = y * lax.rsqrt(jnp.mean(y * y, axis=-1, keepdims=True) + EPS)
        g = g_ref[rows, :].astype(F32)
        gate = g * (1.0 / (1.0 + jnp.exp(-g)))
        o_ref[rows, :] = (gate * yn).astype(o_ref.dtype)
        return carry

    lax.fori_loop(0, seq // c, chunk, 0, unroll=8)


def _retention(proj, batch, seq, n_heads, col0):
    d = RET_HEAD_DIM
    c = RET_CHUNK
    half = d // 2
    blk0 = col0 // d

    pos = jnp.arange(seq)
    inv_freq = ROPE_BASE ** (-jnp.arange(half, dtype=F32) / half)
    ang = pos.astype(F32)[:, None] * inv_freq[None, :]
    cos = jnp.cos(ang)
    sin = jnp.sin(ang)
    log_gamma = jnp.log(1.0 - 2.0 ** (-5.0 - jnp.arange(n_heads, dtype=F32)))
    idx = jnp.arange(c, dtype=F32)
    diff = idx[:, None] - idx[None, :]
    dmask = jnp.where(diff >= 0,
                      jnp.exp(log_gamma[:, None, None] * jnp.maximum(diff, 0.0)),
                      0.0).astype(F32)
    kdec = jnp.exp(log_gamma[:, None] * (c - 1 - idx)).astype(F32)
    qdec = jnp.exp(log_gamma[:, None] * (idx + 1.0)).astype(F32)
    cdec = jnp.exp(log_gamma * c).astype(F32)
    kdec = jnp.broadcast_to(kdec[:, :, None], (n_heads, c, d))
    qdec = jnp.broadcast_to(qdec[:, :, None], (n_heads, c, d))

    def head_spec(group):
        return pl.BlockSpec((seq, d), lambda b, h: (b, blk0 + group * n_heads + h))

    return pl.pallas_call(
        functools.partial(_retention_kernel, seq=seq),
        grid=(batch, n_heads),
        in_specs=[pl.BlockSpec(memory_space=pltpu.SMEM),
                  head_spec(0), head_spec(1), head_spec(2), head_spec(3),
                  pl.BlockSpec((seq, half), lambda b, h: (0, 0)),
                  pl.BlockSpec((seq, half), lambda b, h: (0, 0)),
                  pl.BlockSpec((None, c, c), lambda b, h: (h, 0, 0)),
                  pl.BlockSpec((None, c, d), lambda b, h: (h, 0, 0)),
                  pl.BlockSpec((None, c, d), lambda b, h: (h, 0, 0))],
        out_specs=pl.BlockSpec((seq, d), lambda b, h: (b, h)),
        out_shape=jax.ShapeDtypeStruct((batch * seq, n_heads * d), BF16),
        scratch_shapes=[pltpu.VMEM((d, d), F32)],
        compiler_params=_params(("parallel", "parallel")),
        name="retention",
    )(cdec, proj, proj, proj, proj, cos, sin, dmask, qdec, kdec)


def _t5_bucket(dist):
    n = jnp.maximum(dist, 0)
    max_exact = REL_BUCKETS // 2
    nf = jnp.maximum(n, 1).astype(F32)
    large = max_exact + (jnp.log(nf / max_exact) / math.log(REL_MAX_DIST / max_exact)
                         * (REL_BUCKETS - max_exact)).astype(jnp.int32)
    large = jnp.minimum(large, REL_BUCKETS - 1)
    return jnp.where(n < max_exact, n, large)


def _t5_bucket_static(n):
    n = np.maximum(n, 0)
    max_exact = REL_BUCKETS // 2
    nf = np.maximum(n, 1).astype(np.float32)
    large = max_exact + (np.log(nf / max_exact) / math.log(REL_MAX_DIST / max_exact)
                         * (REL_BUCKETS - max_exact)).astype(np.int32)
    large = np.minimum(large, REL_BUCKETS - 1)
    return np.where(n < max_exact, n, large)


def _reduce_rows(pair_op, final_op, x, rows=32):
    slabs = [x[i:i + rows, :] for i in range(0, x.shape[0], rows)]
    while len(slabs) > 1:
        nxt = [pair_op(slabs[i], slabs[i + 1]) for i in range(0, len(slabs) - 1, 2)]
        if len(slabs) % 2:
            nxt.append(slabs[-1])
        slabs = nxt
    return final_op(slabs[0], axis=0, keepdims=True)


def _moba_kernel(cfar_ref, q_ref, k_ref, v_ref, wrev_ref, o_ref,
                 t0_ref, t1_ref, qa_ref, ka_ref, vt_ref, s_ref, p_ref, *, seq):
    h = pl.program_id(0)
    b = pl.program_id(1)
    blk = MOBA_BLOCK
    d = MOBA_HEAD_DIM
    nb = seq // blk
    cfar = cfar_ref[h]

    @pl.when(b == 0)
    def _():
        w2 = jnp.broadcast_to(wrev_ref[...], (blk, 2 * blk))
        t0 = pltpu.roll(w2, 0, 1, stride=1, stride_axis=0)[:, :blk]
        t1 = pltpu.roll(w2, blk, 1, stride=1, stride_axis=0)[:, :blk]
        row = lax.broadcasted_iota(jnp.int32, (blk, blk), 0)
        col = lax.broadcasted_iota(jnp.int32, (blk, blk), 1)
        t0_ref[...] = jnp.where(row >= col, (t0 - cfar) * LOG2E, NEG_INF).T
        t1_ref[...] = ((t1 - cfar) * LOG2E).T

    kf = k_ref[...].astype(F32)
    kmean = jnp.concatenate(
        [jnp.mean(kf[j * blk:(j + 1) * blk, :], axis=0, keepdims=True) for j in range(nb)],
        axis=0)
    gate = lax.dot_general(kmean.astype(BF16), q_ref[...], NT_DIMS,
                           preferred_element_type=F32)
    jidx = lax.broadcasted_iota(jnp.int32, (nb, seq), 0)
    qblk = lax.broadcasted_iota(jnp.int32, (nb, seq), 1) // blk
    rank = jnp.zeros((nb, seq), jnp.int32)
    for jp in range(nb):
        rowg = gate[jp:jp + 1, :]
        beats = jnp.logical_or(rowg > gate,
                               jnp.logical_and(rowg == gate, jp < jidx))
        rank = rank + jnp.where(jnp.logical_and(beats, jp < qblk), 1, 0)
    keep = jnp.logical_or(jidx >= qblk, rank < MOBA_TOPK)
    negmask = jnp.where(keep, 0.0, NEG_INF).astype(BF16)
    eye = (lax.broadcasted_iota(jnp.int32, (nb, LANES), 0) ==
           lax.broadcasted_iota(jnp.int32, (nb, LANES), 1)).astype(BF16)
    mask_cols = lax.dot_general(negmask, eye, TN_DIMS,
                                preferred_element_type=F32)

    qa_ref[:, :d] = (q_ref[...].astype(F32) * (d ** -0.5 * LOG2E)).astype(BF16)
    qa_ref[:, d:] = mask_cols.astype(BF16)
    ka_ref[:, :d] = k_ref[...]
    ka_ref[:, d:] = (lax.broadcasted_iota(jnp.int32, (seq, LANES), 0) // blk ==
                     lax.broadcasted_iota(jnp.int32, (seq, LANES), 1)).astype(BF16)

    vt_ref[...] = v_ref[...].astype(F32).T.astype(BF16)

    t0 = t0_ref[...]
    t1 = t1_ref[...]
    def scores(qb):
        r0 = qb * blk
        w = r0 + blk
        s = lax.dot_general(ka_ref[0:w, :], qa_ref[r0:w, :], NT_DIMS,
                            preferred_element_type=F32)
        slot = qb % 2
        s_ref[slot, w - blk:w, :] = s[w - blk:, :] + t0
        if qb >= 1:
            s_ref[slot, w - 2 * blk:w - blk, :] = s[w - 2 * blk:w - blk, :] + t1
        if qb >= 2:
            s_ref[slot, 0:w - 2 * blk, :] = s[:w - 2 * blk, :]

    def softmax(qb):
        w = (qb + 1) * blk
        s = s_ref[qb % 2, 0:w, :]
        m = _reduce_rows(jnp.maximum, jnp.max, s)
        p = jnp.exp2(s - m)
        p_ref[qb % 2, 0:w, :] = p.astype(BF16)
        return _reduce_rows(jnp.add, jnp.sum, p)

    def values(qb, l):
        r0 = qb * blk
        w = r0 + blk
        o = jnp.dot(vt_ref[:, 0:w], p_ref[qb % 2, 0:w, :],
                    preferred_element_type=F32)
        o_ref[r0:w, :] = (o / l).T.astype(o_ref.dtype)

    scores(0)
    row_sum = softmax(0)
    if nb > 1:
        scores(1)
    for qb in range(nb):
        if qb + 2 < nb:
            scores(qb + 2)
        next_sum = softmax(qb + 1) if qb + 1 < nb else None
        values(qb, row_sum)
        row_sum = next_sum


def _moba(proj, rel_bias, batch, seq, n_heads, col0):
    d = MOBA_HEAD_DIM
    blk = MOBA_BLOCK
    blk0 = col0 // d
    assert d == LANES and seq // blk <= LANES

    far = _t5_bucket_static(np.arange(blk + 1, max(seq, blk + 2)))
    assert (far == far[0]).all(), "far-block bias is not constant"
    bvec = rel_bias[_t5_bucket(jnp.arange(2 * blk))].T.astype(F32)
    cfar = rel_bias[_t5_bucket(jnp.arange(blk + 1, blk + 2))[0]].astype(F32)
    wrev = jnp.roll(bvec[:, ::-1], 1, axis=1).reshape(n_heads, 1, 2 * blk)

    def head_spec(group):
        return pl.BlockSpec((seq, d), lambda h, b: (b, blk0 + group * n_heads + h))

    return pl.pallas_call(
        functools.partial(_moba_kernel, seq=seq),
        grid=(n_heads, batch),
        in_specs=[pl.BlockSpec(memory_space=pltpu.SMEM),
                  head_spec(0), head_spec(1), head_spec(2),
                  pl.BlockSpec((None, 1, 2 * blk), lambda h, b: (h, 0, 0))],
        out_specs=pl.BlockSpec((seq, d), lambda h, b: (b, h)),
        out_shape=jax.ShapeDtypeStruct((batch * seq, n_heads * d), BF16),
        scratch_shapes=[pltpu.VMEM((blk, blk), F32), pltpu.VMEM((blk, blk), F32),
                        pltpu.VMEM((seq, d + LANES), BF16),
                        pltpu.VMEM((seq, d + LANES), BF16),
                        pltpu.VMEM((d, seq), BF16),
                        pltpu.VMEM((2, seq, blk), F32),
                        pltpu.VMEM((2, seq, blk), BF16)],
        compiler_params=_params(("arbitrary", "arbitrary")),
        name="moba",
    )(cfar, proj, proj, proj, wrev)


def kernel(x, norm_mix, w_in, rel_bias, w_out, norm_ffn, w_up, w_down, norm_final):
    batch, seq, d_model = x.shape
    depth = w_in.shape[0]
    n_moba = rel_bias.shape[1]
    moba_width = n_moba * MOBA_HEAD_DIM
    mix_width = w_out.shape[1]
    ret_width = mix_width - moba_width
    n_ret = ret_width // RET_HEAD_DIM
    assert w_in.shape[2] == 4 * ret_width + 3 * moba_width
    assert seq % MOBA_BLOCK == 0 and seq % RET_CHUNK == 0

    xs = x.reshape(batch * seq, d_model)
    for l in range(depth):
        h = _rmsnorm(xs, norm_mix[l], BF16)
        proj = _matmul([h], w_in[l], out_dtype=BF16, name="in_proj", **FULL_K_TILES)
        y_ret = _retention(proj, batch, seq, n_ret, 0)
        y_moba = _moba(proj, rel_bias, batch, seq, n_moba, 4 * ret_width)
        xs, hg, ssq = _matmul([y_ret, y_moba], w_out[l], out_dtype=F32, res=xs,
                              prenorm_gain=norm_ffn[l], name="out_proj",
                              **FULL_K_RES_TILES)
        a = _matmul([hg], w_up[l], out_dtype=BF16, act="relu2", name="ffn_up",
                    **FULL_K_TILES)
        xs = _matmul([a], w_down[l], out_dtype=F32, res=xs, row_ssq=ssq,
                     norm_dim=d_model, name="ffn_down", **SPLIT_K_TILES)
    out = _rmsnorm(xs, norm_final, F32)
    return out.reshape(batch, seq, d_model)
```

```python
import functools
import math

import numpy as np
import jax
import jax.numpy as jnp
from jax import lax
from jax.experimental import pallas as pl
from jax.experimental.pallas import tpu as pltpu

F32 = jnp.float32
BF16 = jnp.bfloat16

RET_HEAD_DIM = 256
RET_CHUNK = 128
ROPE_BASE = 10000.0
MOBA_HEAD_DIM = 128
MOBA_BLOCK = 256
MOBA_TOPK = 3
REL_BUCKETS = 32
REL_MAX_DIST = 128
EPS = 1e-6
NEG_INF = -1e30
LOG2E = math.log2(math.e)

V7X_VMEM_BYTES = 64 * 1024 * 1024
VMEM_LIMIT_BYTES = V7X_VMEM_BYTES - 4 * 1024 * 1024
LANES = 128

FULL_K_TILES = dict(tm=2048, tn=512, lhs_buffers=1)
FULL_K_RES_TILES = dict(tm=1024, tn=512)
SPLIT_K_TILES = dict(tm=2048, tn=1024, tk=1024)

NT_DIMS = (((1,), (1,)), ((), ()))
TN_DIMS = (((0,), (0,)), ((), ()))


def _params(semantics):
    return pltpu.CompilerParams(dimension_semantics=semantics,
                                vmem_limit_bytes=VMEM_LIMIT_BYTES)


def _rmsnorm_kernel(x_ref, g_ref, o_ref):
    x = x_ref[...]
    ms = jnp.mean(x * x, axis=-1, keepdims=True)
    o_ref[...] = (x * lax.rsqrt(ms + EPS) * g_ref[...]).astype(o_ref.dtype)


def _rmsnorm(x, g, out_dtype, rows=512):
    m, d = x.shape
    return pl.pallas_call(
        _rmsnorm_kernel,
        grid=(m // rows,),
        in_specs=[pl.BlockSpec((rows, d), lambda i: (i, 0)),
                  pl.BlockSpec((1, d), lambda i: (0, 0))],
        out_specs=pl.BlockSpec((rows, d), lambda i: (i, 0)),
        out_shape=jax.ShapeDtypeStruct((m, d), out_dtype),
        compiler_params=_params(("parallel",)),
        name="rmsnorm",
    )(x, g.reshape(1, d).astype(F32))


def _matmul_kernel(*refs, n_lhs, nk, act, has_res, emit_prenorm, has_row_ssq, norm_dim):
    refs = list(refs)
    x_refs = [refs.pop(0) for _ in range(n_lhs)]
    w_ref = refs.pop(0)
    res_ref = refs.pop(0) if has_res else None
    gain_ref = refs.pop(0) if emit_prenorm else None
    ssq_in_ref = refs.pop(0) if has_row_ssq else None
    o_ref = refs.pop(0)
    if emit_prenorm:
        hg_ref, ssq_ref = refs

    def product():
        acc, k0 = None, 0
        for x_ref in x_refs:
            k1 = k0 + x_ref.shape[1]
            part = jnp.dot(x_ref[...], w_ref[k0:k1, :].astype(BF16),
                           preferred_element_type=F32)
            acc = part if acc is None else acc + part
            k0 = k1
        return acc

    def row_factor(width):
        r2 = 1.0 / (ssq_in_ref[...] / norm_dim + EPS)
        return jnp.concatenate([r2] * (width // LANES), axis=1)

    if nk == 1:
        acc = product()
        if act == "relu2":
            acc = jnp.square(jnp.maximum(acc, 0.0))
        if ssq_in_ref is not None:
            acc = acc * row_factor(acc.shape[1])
        if res_ref is not None:
            acc = res_ref[...] + acc
        o_ref[...] = acc.astype(o_ref.dtype)
        if emit_prenorm:
            hg_ref[...] = (acc * gain_ref[...]).astype(hg_ref.dtype)
            part = jnp.broadcast_to(jnp.sum(acc * acc, axis=1, keepdims=True),
                                    ssq_ref.shape)
            j = pl.program_id(1)

            @pl.when(j == 0)
            def _():
                ssq_ref[...] = part

            @pl.when(j > 0)
            def _():
                ssq_ref[...] += part
        return

    k = pl.program_id(2)

    @pl.when(k == 0)
    def _():
        o_ref[...] = product()

    @pl.when(jnp.logical_and(k > 0, k < nk - 1))
    def _():
        o_ref[...] += product()

    @pl.when(k == nk - 1)
    def _():
        acc = o_ref[...] + product()
        if ssq_in_ref is not None:
            acc = acc * row_factor(acc.shape[1])
        if res_ref is not None:
            acc = res_ref[...] + acc
        o_ref[...] = acc


def _matmul(xs, w, *, out_dtype, act=None, res=None, prenorm_gain=None, row_ssq=None,
            norm_dim=None, tm, tn, tk=None, lhs_buffers=None, name):
    m = xs[0].shape[0]
    kdim, n = w.shape
    assert sum(x.shape[1] for x in xs) == kdim
    tk = kdim if tk is None else tk
    nk = kdim // tk
    assert m % tm == 0 and n % tn == 0 and kdim % tk == 0
    assert nk == 1 or (len(xs) == 1 and out_dtype == F32 and act is None
                       and prenorm_gain is None)
    assert (row_ssq is None) == (norm_dim is None)
    mode = {} if lhs_buffers is None else dict(pipeline_mode=pl.Buffered(lhs_buffers))
    if nk == 1:
        in_specs = [pl.BlockSpec((tm, x.shape[1]), lambda i, j, k: (i, 0), **mode)
                    for x in xs]
    else:
        in_specs = [pl.BlockSpec((tm, tk), lambda i, j, k: (i, k), **mode)]
    in_specs.append(pl.BlockSpec((tk, tn), lambda i, j, k: (k, j)))
    args = [*xs, w]
    if res is not None:
        in_specs.append(pl.BlockSpec((tm, tn), lambda i, j, k: (i, j)))
        args.append(res)
    out_specs = pl.BlockSpec((tm, tn), lambda i, j, k: (i, j))
    out_shape = jax.ShapeDtypeStruct((m, n), out_dtype)
    row_spec = pl.BlockSpec((tm, LANES), lambda i, j, k: (i, 0))
    if prenorm_gain is not None:
        in_specs.append(pl.BlockSpec((1, tn), lambda i, j, k: (0, j)))
        args.append(prenorm_gain.reshape(1, n).astype(F32))
        out_specs = [out_specs, pl.BlockSpec((tm, tn), lambda i, j, k: (i, j)), row_spec]
        out_shape = [out_shape, jax.ShapeDtypeStruct((m, n), BF16),
                     jax.ShapeDtypeStruct((m, LANES), F32)]
    if row_ssq is not None:
        in_specs.append(row_spec)
        args.append(row_ssq)
    return pl.pallas_call(
        functools.partial(_matmul_kernel, n_lhs=len(xs), nk=nk, act=act,
                          has_res=res is not None,
                          emit_prenorm=prenorm_gain is not None,
                          has_row_ssq=row_ssq is not None, norm_dim=norm_dim),
        grid=(m // tm, n // tn, nk),
        in_specs=in_specs,
        out_specs=out_specs,
        out_shape=out_shape,
        compiler_params=_params(("parallel", "arbitrary", "arbitrary")),
        name=name,
    )(*args)


def _retention_kernel(cdec_ref, q_ref, k_ref, v_ref, g_ref, cos_ref, sin_ref,
                      dmask_ref, qdec_ref, kdec_ref, o_ref, state_ref, *, seq):
    h = pl.program_id(1)
    c = RET_CHUNK
    half = RET_HEAD_DIM // 2
    cd = cdec_ref[h]
    k_scale = RET_HEAD_DIM ** -0.5
    state_ref[...] = jnp.zeros_like(state_ref)
    dmask = dmask_ref[...]
    qdec = qdec_ref[...]
    kdec = kdec_ref[...]

    def rot(t, cos, sin):
        t1, t2 = t[:, :half], t[:, half:]
        return jnp.concatenate([t1 * cos - t2 * sin, t1 * sin + t2 * cos], axis=-1)

    def chunk(n, carry):
        r0 = pl.multiple_of(n * c, c)
        rows = pl.ds(r0, c)
        cos = cos_ref[rows, :]
        sin = sin_ref[rows, :]
        qr = rot(q_ref[rows, :].astype(F32), cos, sin)
        kr = rot(k_ref[rows, :].astype(F32), cos, sin) * k_scale
        v = v_ref[rows, :]
        inner = lax.dot_general(qr.astype(BF16), kr.astype(BF16), NT_DIMS,
                                preferred_element_type=F32) * dmask
        y = jnp.dot(inner.astype(BF16), v, preferred_element_type=F32)
        state = state_ref[...]
        y = y + jnp.dot((qr * qdec).astype(BF16), state.astype(BF16),
                        preferred_element_type=F32)
        contrib = lax.dot_general((kr * kdec).astype(BF16), v, TN_DIMS,
                                  preferred_element_type=F32)
        state_ref[...] = state * cd + contrib
        yn = y * lax.rsqrt(jnp.mean(y * y, axis=-1, keepdims=True) + EPS)
        g = g_ref[rows, :].astype(F32)
        gate = g * (1.0 / (1.0 + jnp.exp(-g)))
        o_ref[rows, :] = (gate * yn).astype(o_ref.dtype)
        return carry

    lax.fori_loop(0, seq // c, chunk, 0, unroll=8)


def _retention(proj, batch, seq, n_heads, col0):
    d = RET_HEAD_DIM
    c = RET_CHUNK
    half = d // 2
    blk0 = col0 // d

    pos = jnp.arange(seq)
    inv_freq = ROPE_BASE ** (-jnp.arange(half, dtype=F32) / half)
    ang = pos.astype(F32)[:, None] * inv_freq[None, :]
    cos = jnp.cos(ang)
    sin = jnp.sin(ang)
    log_gamma = jnp.log(1.0 - 2.0 ** (-5.0 - jnp.arange(n_heads, dtype=F32)))
    idx = jnp.arange(c, dtype=F32)
    diff = idx[:, None] - idx[None, :]
    dmask = jnp.where(diff >= 0,
                      jnp.exp(log_gamma[:, None, None] * jnp.maximum(diff, 0.0)),
                      0.0).astype(F32)
    kdec = jnp.exp(log_gamma[:, None] * (c - 1 - idx)).astype(F32)
    qdec = jnp.exp(log_gamma[:, None] * (idx + 1.0)).astype(F32)
    cdec = jnp.exp(log_gamma * c).astype(F32)
    kdec = jnp.broadcast_to(kdec[:, :, None], (n_heads, c, d))
    qdec = jnp.broadcast_to(qdec[:, :, None], (n_heads, c, d))

    def head_spec(group):
        return pl.BlockSpec((seq, d), lambda b, h: (b, blk0 + group * n_heads + h))

    return pl.pallas_call(
        functools.partial(_retention_kernel, seq=seq),
        grid=(batch, n_heads),
        in_specs=[pl.BlockSpec(memory_space=pltpu.SMEM),
                  head_spec(0), head_spec(1), head_spec(2), head_spec(3),
                  pl.BlockSpec((seq, half), lambda b, h: (0, 0)),
                  pl.BlockSpec((seq, half), lambda b, h: (0, 0)),
                  pl.BlockSpec((None, c, c), lambda b, h: (h, 0, 0)),
                  pl.BlockSpec((None, c, d), lambda b, h: (h, 0, 0)),
                  pl.BlockSpec((None, c, d), lambda b, h: (h, 0, 0))],
        out_specs=pl.BlockSpec((seq, d), lambda b, h: (b, h)),
        out_shape=jax.ShapeDtypeStruct((batch * seq, n_heads * d), BF16),
        scratch_shapes=[pltpu.VMEM((d, d), F32)],
        compiler_params=_params(("parallel", "parallel")),
        name="retention",
    )(cdec, proj, proj, proj, proj, cos, sin, dmask, qdec, kdec)


def _t5_bucket(dist):
    n = jnp.maximum(dist, 0)
    max_exact = REL_BUCKETS // 2
    nf = jnp.maximum(n, 1).astype(F32)
    large = max_exact + (jnp.log(nf / max_exact) / math.log(REL_MAX_DIST / max_exact)
                         * (REL_BUCKETS - max_exact)).astype(jnp.int32)
    large = jnp.minimum(large, REL_BUCKETS - 1)
    return jnp.where(n < max_exact, n, large)


def _t5_bucket_static(n):
    n = np.maximum(n, 0)
    max_exact = REL_BUCKETS // 2
    nf = np.maximum(n, 1).astype(np.float32)
    large = max_exact + (np.log(nf / max_exact) / math.log(REL_MAX_DIST / max_exact)
                         * (REL_BUCKETS - max_exact)).astype(np.int32)
    large = np.minimum(large, REL_BUCKETS - 1)
    return np.where(n < max_exact, n, large)


def _reduce_rows(pair_op, final_op, x, rows=32):
    slabs = [x[i:i + rows, :] for i in range(0, x.shape[0], rows)]
    while len(slabs) > 1:
        nxt = [pair_op(slabs[i], slabs[i + 1]) for i in range(0, len(slabs) - 1, 2)]
        if len(slabs) % 2:
            nxt.append(slabs[-1])
        slabs = nxt
    return final_op(slabs[0], axis=0, keepdims=True)


def _moba_kernel(cfar_ref, q_ref, k_ref, v_ref, wrev_ref, o_ref,
                 t0_ref, t1_ref, qa_ref, ka_ref, vt_ref, s_ref, p_ref, *, seq):
    h = pl.program_id(0)
    b = pl.program_id(1)
    blk = MOBA_BLOCK
    d = MOBA_HEAD_DIM
    nb = seq // blk
    cfar = cfar_ref[h]

    @pl.when(b == 0)
    def _():
        w2 = jnp.broadcast_to(wrev_ref[...], (blk, 2 * blk))
        t0 = pltpu.roll(w2, 0, 1, stride=1, stride_axis=0)[:, :blk]
        t1 = pltpu.roll(w2, blk, 1, stride=1, stride_axis=0)[:, :blk]
        row = lax.broadcasted_iota(jnp.int32, (blk, blk), 0)
        col = lax.broadcasted_iota(jnp.int32, (blk, blk), 1)
        t0_ref[...] = jnp.where(row >= col, (t0 - cfar) * LOG2E, NEG_INF).T
        t1_ref[...] = ((t1 - cfar) * LOG2E).T

    qa_ref[:, :d] = (q_ref[...].astype(F32) * (d ** -0.5 * LOG2E)).astype(BF16)
    ka_ref[:, :d] = k_ref[...]
    ka_ref[:, d:] = (lax.broadcasted_iota(jnp.int32, (seq, LANES), 0) // blk ==
                     lax.broadcasted_iota(jnp.int32, (seq, LANES), 1)).astype(BF16)

    free = min((MOBA_TOPK + 1) * blk, seq)
    qa_ref[0:free, d:] = jnp.zeros((free, LANES), BF16)
    if free < seq:
        gated = seq - free
        kf = k_ref[...].astype(F32)
        kmean = jnp.concatenate(
            [jnp.mean(kf[j * blk:(j + 1) * blk, :], axis=0, keepdims=True)
             for j in range(nb)], axis=0)
        gate = lax.dot_general(kmean.astype(BF16), q_ref[free:seq, :], NT_DIMS,
                               preferred_element_type=F32)
        jidx = lax.broadcasted_iota(jnp.int32, (nb, gated), 0)
        qblk = (lax.broadcasted_iota(jnp.int32, (nb, gated), 1) + free) // blk
        rank = jnp.zeros((nb, gated), jnp.int32)
        for jp in range(nb):
            rowg = gate[jp:jp + 1, :]
            beats = jnp.logical_or(rowg > gate,
                                   jnp.logical_and(rowg == gate, jp < jidx))
            rank = rank + jnp.where(jnp.logical_and(beats, jp < qblk), 1, 0)
        keep = jnp.logical_or(jidx >= qblk, rank < MOBA_TOPK)
        negmask = jnp.where(keep, 0.0, NEG_INF).astype(BF16)
        eye = (lax.broadcasted_iota(jnp.int32, (nb, LANES), 0) ==
               lax.broadcasted_iota(jnp.int32, (nb, LANES), 1)).astype(BF16)
        mask_cols = lax.dot_general(negmask, eye, TN_DIMS,
                                    preferred_element_type=F32)
        qa_ref[free:seq, d:] = mask_cols.astype(BF16)

    vt_ref[...] = v_ref[...].astype(F32).T.astype(BF16)

    t0 = t0_ref[...]
    t1 = t1_ref[...]
    def scores(qb, slot):
        r0 = qb * blk
        w = r0 + blk
        s = lax.dot_general(ka_ref[0:w, :], qa_ref[r0:w, :], NT_DIMS,
                            preferred_element_type=F32)
        s_ref[slot, w - blk:w, :] = s[w - blk:, :] + t0
        if qb >= 1:
            s_ref[slot, w - 2 * blk:w - blk, :] = s[w - 2 * blk:w - blk, :] + t1
        if qb >= 2:
            s_ref[slot, 0:w - 2 * blk, :] = s[:w - 2 * blk, :]

    def softmax(qb, slot):
        w = (qb + 1) * blk
        s = s_ref[slot, 0:w, :]
        m = _reduce_rows(jnp.maximum, jnp.max, s)
        p = jnp.exp2(s - m)
        p_ref[slot, 0:w, :] = p.astype(BF16)
        return _reduce_rows(jnp.add, jnp.sum, p)

    def values(qb, slot, l):
        r0 = qb * blk
        w = r0 + blk
        o = jnp.dot(vt_ref[:, 0:w], p_ref[slot, 0:w, :],
                    preferred_element_type=F32)
        o_ref[r0:w, :] = (o / l).T.astype(o_ref.dtype)

    n_free = free // blk
    order = list(range(n_free)) + list(range(nb - 1, n_free - 1, -1))
    assert sorted(order) == list(range(nb))
    scores(order[0], 0)
    row_sum = softmax(order[0], 0)
    if nb > 1:
        scores(order[1], 1)
    for pos in range(nb):
        if pos + 2 < nb:
            scores(order[pos + 2], pos % 2)
        values(order[pos], pos % 2, row_sum)
        row_sum = softmax(order[pos + 1], (pos + 1) % 2) if pos + 1 < nb else None


def _moba(proj, rel_bias, batch, seq, n_heads, col0):
    d = MOBA_HEAD_DIM
    blk = MOBA_BLOCK
    blk0 = col0 // d
    assert d == LANES and seq // blk <= LANES

    far = _t5_bucket_static(np.arange(blk + 1, max(seq, blk + 2)))
    assert (far == far[0]).all(), "far-block bias is not constant"
    bvec = rel_bias[_t5_bucket(jnp.arange(2 * blk))].T.astype(F32)
    cfar = rel_bias[_t5_bucket(jnp.arange(blk + 1, blk + 2))[0]].astype(F32)
    wrev = jnp.roll(bvec[:, ::-1], 1, axis=1).reshape(n_heads, 1, 2 * blk)

    def head_spec(group):
        return pl.BlockSpec((seq, d), lambda h, b: (b, blk0 + group * n_heads + h))

    return pl.pallas_call(
        functools.partial(_moba_kernel, seq=seq),
        grid=(n_heads, batch),
        in_specs=[pl.BlockSpec(memory_space=pltpu.SMEM),
                  head_spec(0), head_spec(1), head_spec(2),
                  pl.BlockSpec((None, 1, 2 * blk), lambda h, b: (h, 0, 0))],
        out_specs=pl.BlockSpec((seq, d), lambda h, b: (b, h)),
        out_shape=jax.ShapeDtypeStruct((batch * seq, n_heads * d), BF16),
        scratch_shapes=[pltpu.VMEM((blk, blk), F32), pltpu.VMEM((blk, blk), F32),
                        pltpu.VMEM((seq, d + LANES), BF16),
                        pltpu.VMEM((seq, d + LANES), BF16),
                        pltpu.VMEM((d, seq), BF16),
                        pltpu.VMEM((2, seq, blk), F32),
                        pltpu.VMEM((2, seq, blk), BF16)],
        compiler_params=_params(("arbitrary", "arbitrary")),
        name="moba",
    )(cfar, proj, proj, proj, wrev)


def kernel(x, norm_mix, w_in, rel_bias, w_out, norm_ffn, w_up, w_down, norm_final):
    batch, seq, d_model = x.shape
    depth = w_in.shape[0]
    n_moba = rel_bias.shape[1]
    moba_width = n_moba * MOBA_HEAD_DIM
    mix_width = w_out.shape[1]
    ret_width = mix_width - moba_width
    n_ret = ret_width // RET_HEAD_DIM
    assert w_in.shape[2] == 4 * ret_width + 3 * moba_width
    assert seq % MOBA_BLOCK == 0 and seq % RET_CHUNK == 0

    xs = x.reshape(batch * seq, d_model)
    for l in range(depth):
        h = _rmsnorm(xs, norm_mix[l], BF16)
        proj = _matmul([h], w_in[l], out_dtype=BF16, name="in_proj", **FULL_K_TILES)
        y_ret = _retention(proj, batch, seq, n_ret, 0)
        y_moba = _moba(proj, rel_bias, batch, seq, n_moba, 4 * ret_width)
        xs, hg, ssq = _matmul([y_ret, y_moba], w_out[l], out_dtype=F32, res=xs,
                              prenorm_gain=norm_ffn[l], name="out_proj",
                              **FULL_K_RES_TILES)
        a = _matmul([hg], w_up[l], out_dtype=BF16, act="relu2", name="ffn_up",
                    **FULL_K_TILES)
        xs = _matmul([a], w_down[l], out_dtype=F32, res=xs, row_ssq=ssq,
                     norm_dim=d_model, name="ffn_down", **SPLIT_K_TILES)
    out = _rmsnorm(xs, norm_final, F32)
    return out.reshape(batch, seq, d_model)
```

```python
import functools
import math

import numpy as np
import jax
import jax.numpy as jnp
from jax import lax
from jax.experimental import pallas as pl
from jax.experimental.pallas import tpu as pltpu

F32 = jnp.float32
BF16 = jnp.bfloat16

RET_HEAD_DIM = 256
RET_CHUNK = 128
ROPE_BASE = 10000.0
MOBA_HEAD_DIM = 128
MOBA_BLOCK = 256
MOBA_TOPK = 3
REL_BUCKETS = 32
REL_MAX_DIST = 128
EPS = 1e-6
NEG_INF = -1e30
LOG2E = math.log2(math.e)

V7X_VMEM_BYTES = 64 * 1024 * 1024
VMEM_LIMIT_BYTES = V7X_VMEM_BYTES - 4 * 1024 * 1024
LANES = 128

F32_WEIGHT_TILES = dict(tm=2048, tn=512, lhs_buffers=1)
BF16_WEIGHT_TILES = dict(tm=2048, tn=1024, lhs_buffers=1)
BF16_WEIGHT_RES_TILES = dict(tm=2048, tn=512, lhs_buffers=1)
SPLIT_K_TILES = dict(tm=1024, tn=1024, tk=4096)

NT_DIMS = (((1,), (1,)), ((), ()))
TN_DIMS = (((0,), (0,)), ((), ()))


def _params(semantics):
    return pltpu.CompilerParams(dimension_semantics=semantics,
                                vmem_limit_bytes=VMEM_LIMIT_BYTES)


def _rmsnorm_kernel(x_ref, g_ref, o_ref):
    x = x_ref[...]
    ms = jnp.mean(x * x, axis=-1, keepdims=True)
    o_ref[...] = (x * lax.rsqrt(ms + EPS) * g_ref[...]).astype(o_ref.dtype)


def _rmsnorm(x, g, out_dtype, rows=512):
    m, d = x.shape
    return pl.pallas_call(
        _rmsnorm_kernel,
        grid=(m // rows,),
        in_specs=[pl.BlockSpec((rows, d), lambda i: (i, 0)),
                  pl.BlockSpec((1, d), lambda i: (0, 0))],
        out_specs=pl.BlockSpec((rows, d), lambda i: (i, 0)),
        out_shape=jax.ShapeDtypeStruct((m, d), out_dtype),
        compiler_params=_params(("parallel",)),
        name="rmsnorm",
    )(x, g.reshape(1, d).astype(F32))


def _matmul_kernel(*refs, n_lhs, nk, act, has_res, emit_prenorm, has_row_ssq, norm_dim):
    refs = list(refs)
    x_refs = [refs.pop(0) for _ in range(n_lhs)]
    w_ref = refs.pop(0)
    res_ref = refs.pop(0) if has_res else None
    gain_ref = refs.pop(0) if emit_prenorm else None
    ssq_in_ref = refs.pop(0) if has_row_ssq else None
    o_ref = refs.pop(0)
    if emit_prenorm:
        hg_ref, ssq_ref = refs

    def product():
        acc, k0 = None, 0
        for x_ref in x_refs:
            k1 = k0 + x_ref.shape[1]
            part = jnp.dot(x_ref[...], w_ref[k0:k1, :].astype(BF16),
                           preferred_element_type=F32)
            acc = part if acc is None else acc + part
            k0 = k1
        return acc

    def row_factor(width):
        r2 = 1.0 / (ssq_in_ref[...] / norm_dim + EPS)
        return jnp.concatenate([r2] * (width // LANES), axis=1)

    if nk == 1:
        acc = product()
        if act == "relu2":
            acc = jnp.square(jnp.maximum(acc, 0.0))
        if ssq_in_ref is not None:
            acc = acc * row_factor(acc.shape[1])
        if res_ref is not None:
            acc = res_ref[...] + acc
        o_ref[...] = acc.astype(o_ref.dtype)
        if emit_prenorm:
            hg_ref[...] = (acc * gain_ref[...]).astype(hg_ref.dtype)
            part = jnp.broadcast_to(jnp.sum(acc * acc, axis=1, keepdims=True),
                                    ssq_ref.shape)
            j = pl.program_id(1)

            @pl.when(j == 0)
            def _():
                ssq_ref[...] = part

            @pl.when(j > 0)
            def _():
                ssq_ref[...] += part
        return

    k = pl.program_id(2)

    @pl.when(k == 0)
    def _():
        o_ref[...] = product()

    @pl.when(jnp.logical_and(k > 0, k < nk - 1))
    def _():
        o_ref[...] += product()

    @pl.when(k == nk - 1)
    def _():
        acc = o_ref[...] + product()
        if ssq_in_ref is not None:
            acc = acc * row_factor(acc.shape[1])
        if res_ref is not None:
            acc = res_ref[...] + acc
        o_ref[...] = acc


def _matmul(xs, w, *, out_dtype, act=None, res=None, prenorm_gain=None, row_ssq=None,
            norm_dim=None, tm, tn, tk=None, lhs_buffers=None, name):
    m = xs[0].shape[0]
    kdim, n = w.shape
    assert sum(x.shape[1] for x in xs) == kdim
    tk = kdim if tk is None else tk
    nk = kdim // tk
    assert m % tm == 0 and n % tn == 0 and kdim % tk == 0
    assert nk == 1 or (len(xs) == 1 and out_dtype == F32 and act is None
                       and prenorm_gain is None)
    assert (row_ssq is None) == (norm_dim is None)
    mode = {} if lhs_buffers is None else dict(pipeline_mode=pl.Buffered(lhs_buffers))
    if nk == 1:
        in_specs = [pl.BlockSpec((tm, x.shape[1]), lambda i, j, k: (i, 0), **mode)
                    for x in xs]
    else:
        in_specs = [pl.BlockSpec((tm, tk), lambda i, j, k: (i, k), **mode)]
    in_specs.append(pl.BlockSpec((tk, tn), lambda i, j, k: (k, j)))
    args = [*xs, w]
    if res is not None:
        in_specs.append(pl.BlockSpec((tm, tn), lambda i, j, k: (i, j)))
        args.append(res)
    out_specs = pl.BlockSpec((tm, tn), lambda i, j, k: (i, j))
    out_shape = jax.ShapeDtypeStruct((m, n), out_dtype)
    row_spec = pl.BlockSpec((tm, LANES), lambda i, j, k: (i, 0))
    if prenorm_gain is not None:
        in_specs.append(pl.BlockSpec((1, tn), lambda i, j, k: (0, j)))
        args.append(prenorm_gain.reshape(1, n).astype(F32))
        out_specs = [out_specs, pl.BlockSpec((tm, tn), lambda i, j, k: (i, j)), row_spec]
        out_shape = [out_shape, jax.ShapeDtypeStruct((m, n), BF16),
                     jax.ShapeDtypeStruct((m, LANES), F32)]
    if row_ssq is not None:
        in_specs.append(row_spec)
        args.append(row_ssq)
    return pl.pallas_call(
        functools.partial(_matmul_kernel, n_lhs=len(xs), nk=nk, act=act,
                          has_res=res is not None,
                          emit_prenorm=prenorm_gain is not None,
                          has_row_ssq=row_ssq is not None, norm_dim=norm_dim),
        grid=(m // tm, n // tn, nk),
        in_specs=in_specs,
        out_specs=out_specs,
        out_shape=out_shape,
        compiler_params=_params(("parallel", "arbitrary", "arbitrary")),
        name=name,
    )(*args)


def _rider_specs(weights, n_steps, step_of):
    specs, shapes = [], []
    for w in weights:
        rows, cols = w.shape
        assert rows % n_steps == 0 and (rows // n_steps) % 16 == 0
        specs.append(pl.BlockSpec((rows // n_steps, cols),
                                  lambda *g: (step_of(*g), 0)))
        shapes.append(jax.ShapeDtypeStruct((rows, cols), BF16))
    return specs, shapes


def _cast_riders(in_refs, out_refs):
    for src, dst in zip(in_refs, out_refs):
        dst[...] = src[...].astype(dst.dtype)


def _retention_kernel(*refs, seq, n_riders):
    (cdec_ref, q_ref, k_ref, v_ref, g_ref, cos_ref, sin_ref,
     dmask_ref, qdec_ref, kdec_ref) = refs[:10]
    o_ref = refs[10 + n_riders]
    state_ref = refs[-1]
    _cast_riders(refs[10:10 + n_riders], refs[11 + n_riders:11 + 2 * n_riders])
    h = pl.program_id(1)
    c = RET_CHUNK
    half = RET_HEAD_DIM // 2
    cd = cdec_ref[h]
    k_scale = RET_HEAD_DIM ** -0.5
    state_ref[...] = jnp.zeros_like(state_ref)
    dmask = dmask_ref[...]
    qdec = qdec_ref[...]
    kdec = kdec_ref[...]

    def rot(t, cos, sin):
        t1, t2 = t[:, :half], t[:, half:]
        return jnp.concatenate([t1 * cos - t2 * sin, t1 * sin + t2 * cos], axis=-1)

    def chunk(n, carry):
        r0 = pl.multiple_of(n * c, c)
        rows = pl.ds(r0, c)
        cos = cos_ref[rows, :]
        sin = sin_ref[rows, :]
        qr = rot(q_ref[rows, :].astype(F32), cos, sin)
        kr = rot(k_ref[rows, :].astype(F32), cos, sin) * k_scale
        v = v_ref[rows, :]
        inner = lax.dot_general(qr.astype(BF16), kr.astype(BF16), NT_DIMS,
                                preferred_element_type=F32) * dmask
        y = jnp.dot(inner.astype(BF16), v, preferred_element_type=F32)
        state = state_ref[...]
        y = y + jnp.dot((qr * qdec).astype(BF16), state.astype(BF16),
                        preferred_element_type=F32)
        contrib = lax.dot_general((kr * kdec).astype(BF16), v, TN_DIMS,
                                  preferred_element_type=F32)
        state_ref[...] = state * cd + contrib
        yn = y * lax.rsqrt(jnp.mean(y * y, axis=-1, keepdims=True) + EPS)
        g = g_ref[rows, :].astype(F32)
        gate = g * (1.0 / (1.0 + jnp.exp(-g)))
        o_ref[rows, :] = (gate * yn).astype(o_ref.dtype)
        return carry

    lax.fori_loop(0, seq // c, chunk, 0, unroll=8)


def _retention(proj, batch, seq, n_heads, col0, riders):
    d = RET_HEAD_DIM
    c = RET_CHUNK
    half = d // 2
    blk0 = col0 // d

    pos = jnp.arange(seq)
    inv_freq = ROPE_BASE ** (-jnp.arange(half, dtype=F32) / half)
    ang = pos.astype(F32)[:, None] * inv_freq[None, :]
    cos = jnp.cos(ang)
    sin = jnp.sin(ang)
    log_gamma = jnp.log(1.0 - 2.0 ** (-5.0 - jnp.arange(n_heads, dtype=F32)))
    idx = jnp.arange(c, dtype=F32)
    diff = idx[:, None] - idx[None, :]
    dmask = jnp.where(diff >= 0,
                      jnp.exp(log_gamma[:, None, None] * jnp.maximum(diff, 0.0)),
                      0.0).astype(F32)
    kdec = jnp.exp(log_gamma[:, None] * (c - 1 - idx)).astype(F32)
    qdec = jnp.exp(log_gamma[:, None] * (idx + 1.0)).astype(F32)
    cdec = jnp.exp(log_gamma * c).astype(F32)
    kdec = jnp.broadcast_to(kdec[:, :, None], (n_heads, c, d))
    qdec = jnp.broadcast_to(qdec[:, :, None], (n_heads, c, d))

    def head_spec(group):
        return pl.BlockSpec((seq, d), lambda b, h: (b, blk0 + group * n_heads + h))

    rider_specs, rider_shapes = _rider_specs(riders, batch * n_heads,
                                             lambda b, h: b * n_heads + h)
    return pl.pallas_call(
        functools.partial(_retention_kernel, seq=seq, n_riders=len(riders)),
        grid=(batch, n_heads),
        in_specs=[pl.BlockSpec(memory_space=pltpu.SMEM),
                  head_spec(0), head_spec(1), head_spec(2), head_spec(3),
                  pl.BlockSpec((seq, half), lambda b, h: (0, 0)),
                  pl.BlockSpec((seq, half), lambda b, h: (0, 0)),
                  pl.BlockSpec((None, c, c), lambda b, h: (h, 0, 0)),
                  pl.BlockSpec((None, c, d), lambda b, h: (h, 0, 0)),
                  pl.BlockSpec((None, c, d), lambda b, h: (h, 0, 0)),
                  *rider_specs],
        out_specs=[pl.BlockSpec((seq, d), lambda b, h: (b, h)), *rider_specs],
        out_shape=[jax.ShapeDtypeStruct((batch * seq, n_heads * d), BF16), *rider_shapes],
        scratch_shapes=[pltpu.VMEM((d, d), F32)],
        compiler_params=_params(("parallel", "parallel")),
        name="retention",
    )(cdec, proj, proj, proj, proj, cos, sin, dmask, qdec, kdec, *riders)


def _t5_bucket(dist):
    n = jnp.maximum(dist, 0)
    max_exact = REL_BUCKETS // 2
    nf = jnp.maximum(n, 1).astype(F32)
    large = max_exact + (jnp.log(nf / max_exact) / math.log(REL_MAX_DIST / max_exact)
                         * (REL_BUCKETS - max_exact)).astype(jnp.int32)
    large = jnp.minimum(large, REL_BUCKETS - 1)
    return jnp.where(n < max_exact, n, large)


def _t5_bucket_static(n):
    n = np.maximum(n, 0)
    max_exact = REL_BUCKETS // 2
    nf = np.maximum(n, 1).astype(np.float32)
    large = max_exact + (np.log(nf / max_exact) / math.log(REL_MAX_DIST / max_exact)
                         * (REL_BUCKETS - max_exact)).astype(np.int32)
    large = np.minimum(large, REL_BUCKETS - 1)
    return np.where(n < max_exact, n, large)


def _reduce_rows(pair_op, final_op, x, rows=32):
    slabs = [x[i:i + rows, :] for i in range(0, x.shape[0], rows)]
    while len(slabs) > 1:
        nxt = [pair_op(slabs[i], slabs[i + 1]) for i in range(0, len(slabs) - 1, 2)]
        if len(slabs) % 2:
            nxt.append(slabs[-1])
        slabs = nxt
    return final_op(slabs[0], axis=0, keepdims=True)


def _moba_kernel(*refs, seq, n_riders):
    cfar_ref, q_ref, k_ref, v_ref, wrev_ref = refs[:5]
    o_ref = refs[5 + n_riders]
    t0_ref, t1_ref, qa_ref, ka_ref, vt_ref, s_ref, p_ref = refs[6 + 2 * n_riders:]
    _cast_riders(refs[5:5 + n_riders], refs[6 + n_riders:6 + 2 * n_riders])
    h = pl.program_id(0)
    b = pl.program_id(1)
    blk = MOBA_BLOCK
    d = MOBA_HEAD_DIM
    nb = seq // blk
    cfar = cfar_ref[h]

    @pl.when(b == 0)
    def _():
        w2 = jnp.broadcast_to(wrev_ref[...], (blk, 2 * blk))
        t0 = pltpu.roll(w2, 0, 1, stride=1, stride_axis=0)[:, :blk]
        t1 = pltpu.roll(w2, blk, 1, stride=1, stride_axis=0)[:, :blk]
        row = lax.broadcasted_iota(jnp.int32, (blk, blk), 0)
        col = lax.broadcasted_iota(jnp.int32, (blk, blk), 1)
        t0_ref[...] = jnp.where(row >= col, (t0 - cfar) * LOG2E, NEG_INF).T
        t1_ref[...] = ((t1 - cfar) * LOG2E).T

    qa_ref[:, :d] = (q_ref[...].astype(F32) * (d ** -0.5 * LOG2E)).astype(BF16)
    ka_ref[:, :d] = k_ref[...]
    ka_ref[:, d:] = (lax.broadcasted_iota(jnp.int32, (seq, LANES), 0) // blk ==
                     lax.broadcasted_iota(jnp.int32, (seq, LANES), 1)).astype(BF16)

    free = min((MOBA_TOPK + 1) * blk, seq)
    qa_ref[0:free, d:] = jnp.zeros((free, LANES), BF16)
    if free < seq:
        gated = seq - free
        kf = k_ref[...].astype(F32)
        kmean = jnp.concatenate(
            [jnp.mean(kf[j * blk:(j + 1) * blk, :], axis=0, keepdims=True)
             for j in range(nb)], axis=0)
        gate = lax.dot_general(kmean.astype(BF16), q_ref[free:seq, :], NT_DIMS,
                               preferred_element_type=F32)
        jidx = lax.broadcasted_iota(jnp.int32, (nb, gated), 0)
        qblk = (lax.broadcasted_iota(jnp.int32, (nb, gated), 1) + free) // blk
        rank = jnp.zeros((nb, gated), jnp.int32)
        for jp in range(nb):
            rowg = gate[jp:jp + 1, :]
            beats = jnp.logical_or(rowg > gate,
                                   jnp.logical_and(rowg == gate, jp < jidx))
            rank = rank + jnp.where(jnp.logical_and(beats, jp < qblk), 1, 0)
        keep = jnp.logical_or(jidx >= qblk, rank < MOBA_TOPK)
        negmask = jnp.where(keep, 0.0, NEG_INF).astype(BF16)
        eye = (lax.broadcasted_iota(jnp.int32, (nb, LANES), 0) ==
               lax.broadcasted_iota(jnp.int32, (nb, LANES), 1)).astype(BF16)
        mask_cols = lax.dot_general(negmask, eye, TN_DIMS,
                                    preferred_element_type=F32)
        qa_ref[free:seq, d:] = mask_cols.astype(BF16)

    vt_ref[...] = v_ref[...].astype(F32).T.astype(BF16)

    t0 = t0_ref[...]
    t1 = t1_ref[...]
    def scores(qb, slot):
        r0 = qb * blk
        w = r0 + blk
        s = lax.dot_general(ka_ref[0:w, :], qa_ref[r0:w, :], NT_DIMS,
                            preferred_element_type=F32)
        s_ref[slot, w - blk:w, :] = s[w - blk:, :] + t0
        if qb >= 1:
            s_ref[slot, w - 2 * blk:w - blk, :] = s[w - 2 * blk:w - blk, :] + t1
        if qb >= 2:
            s_ref[slot, 0:w - 2 * blk, :] = s[:w - 2 * blk, :]

    def softmax(qb, slot):
        w = (qb + 1) * blk
        s = s_ref[slot, 0:w, :]
        m = _reduce_rows(jnp.maximum, jnp.max, s)
        p = jnp.exp2(s - m)
        p_ref[slot, 0:w, :] = p.astype(BF16)
        return _reduce_rows(jnp.add, jnp.sum, p)

    def values(qb, slot, l):
        r0 = qb * blk
        w = r0 + blk
        o = jnp.dot(vt_ref[:, 0:w], p_ref[slot, 0:w, :],
                    preferred_element_type=F32)
        o_ref[r0:w, :] = (o / l).T.astype(o_ref.dtype)

    n_free = free // blk
    order = list(range(n_free)) + list(range(nb - 1, n_free - 1, -1))
    assert sorted(order) == list(range(nb))
    scores(order[0], 0)
    row_sum = softmax(order[0], 0)
    if nb > 1:
        scores(order[1], 1)
    for pos in range(nb):
        if pos + 2 < nb:
            scores(order[pos + 2], pos % 2)
        values(order[pos], pos % 2, row_sum)
        row_sum = softmax(order[pos + 1], (pos + 1) % 2) if pos + 1 < nb else None


def _moba(proj, rel_bias, batch, seq, n_heads, col0, riders):
    d = MOBA_HEAD_DIM
    blk = MOBA_BLOCK
    blk0 = col0 // d
    assert d == LANES and seq // blk <= LANES

    far = _t5_bucket_static(np.arange(blk + 1, max(seq, blk + 2)))
    assert (far == far[0]).all(), "far-block bias is not constant"
    bvec = rel_bias[_t5_bucket(jnp.arange(2 * blk))].T.astype(F32)
    cfar = rel_bias[_t5_bucket(jnp.arange(blk + 1, blk + 2))[0]].astype(F32)
    wrev = jnp.roll(bvec[:, ::-1], 1, axis=1).reshape(n_heads, 1, 2 * blk)

    def head_spec(group):
        return pl.BlockSpec((seq, d), lambda h, b: (b, blk0 + group * n_heads + h))

    rider_specs, rider_shapes = _rider_specs(riders, n_heads * batch,
                                             lambda h, b: h * batch + b)
    return pl.pallas_call(
        functools.partial(_moba_kernel, seq=seq, n_riders=len(riders)),
        grid=(n_heads, batch),
        in_specs=[pl.BlockSpec(memory_space=pltpu.SMEM),
                  head_spec(0), head_spec(1), head_spec(2),
                  pl.BlockSpec((None, 1, 2 * blk), lambda h, b: (h, 0, 0)),
                  *rider_specs],
        out_specs=[pl.BlockSpec((seq, d), lambda h, b: (b, h)), *rider_specs],
        out_shape=[jax.ShapeDtypeStruct((batch * seq, n_heads * d), BF16), *rider_shapes],
        scratch_shapes=[pltpu.VMEM((blk, blk), F32), pltpu.VMEM((blk, blk), F32),
                        pltpu.VMEM((seq, d + LANES), BF16),
                        pltpu.VMEM((seq, d + LANES), BF16),
                        pltpu.VMEM((d, seq), BF16),
                        pltpu.VMEM((2, seq, blk), F32),
                        pltpu.VMEM((2, seq, blk), BF16)],
        compiler_params=_params(("arbitrary", "arbitrary")),
        name="moba",
    )(cfar, proj, proj, proj, wrev, *riders)


def kernel(x, norm_mix, w_in, rel_bias, w_out, norm_ffn, w_up, w_down, norm_final):
    batch, seq, d_model = x.shape
    depth = w_in.shape[0]
    n_moba = rel_bias.shape[1]
    moba_width = n_moba * MOBA_HEAD_DIM
    mix_width = w_out.shape[1]
    ret_width = mix_width - moba_width
    n_ret = ret_width // RET_HEAD_DIM
    assert w_in.shape[2] == 4 * ret_width + 3 * moba_width
    assert seq % MOBA_BLOCK == 0 and seq % RET_CHUNK == 0

    xs = x.reshape(batch * seq, d_model)
    for l in range(depth):
        h = _rmsnorm(xs, norm_mix[l], BF16)
        proj = _matmul([h], w_in[l], out_dtype=BF16, name="in_proj", **F32_WEIGHT_TILES)
        y_ret, w_down_bf = _retention(proj, batch, seq, n_ret, 0, [w_down[l]])
        y_moba, w_up_bf, w_out_bf = _moba(proj, rel_bias, batch, seq, n_moba,
                                          4 * ret_width, [w_up[l], w_out[l]])
        xs, hg, ssq = _matmul([y_ret, y_moba], w_out_bf, out_dtype=F32, res=xs,
                              prenorm_gain=norm_ffn[l], name="out_proj",
                              **BF16_WEIGHT_RES_TILES)
        a = _matmul([hg], w_up_bf, out_dtype=BF16, act="relu2", name="ffn_up",
                    **BF16_WEIGHT_TILES)
        xs = _matmul([a], w_down_bf, out_dtype=F32, res=xs, row_ssq=ssq,
                     norm_dim=d_model, name="ffn_down", **SPLIT_K_TILES)
    out = _rmsnorm(xs, norm_final, F32)
    return out.reshape(batch, seq, d_model)
```

```python
import functools
import math

import numpy as np
import jax
import jax.numpy as jnp
from jax import lax
from jax.experimental import pallas as pl
from jax.experimental.pallas import tpu as pltpu

F32 = jnp.float32
BF16 = jnp.bfloat16

RET_HEAD_DIM = 256
RET_CHUNK = 128
ROPE_BASE = 10000.0
MOBA_HEAD_DIM = 128
MOBA_BLOCK = 256
MOBA_TOPK = 3
REL_BUCKETS = 32
REL_MAX_DIST = 128
EPS = 1e-6
NEG_INF = -1e30
LOG2E = math.log2(math.e)

V7X_VMEM_BYTES = 64 * 1024 * 1024
VMEM_LIMIT_BYTES = V7X_VMEM_BYTES - 4 * 1024 * 1024
LANES = 128

F32_WEIGHT_TILES = dict(tm=2048, tn=512, lhs_buffers=1)
BF16_WEIGHT_TILES = dict(tm=1024, tn=512)
SPLIT_K_TILES = dict(tm=1024, tn=1024, tk=4096)

NT_DIMS = (((1,), (1,)), ((), ()))
TN_DIMS = (((0,), (0,)), ((), ()))


def _params(semantics):
    return pltpu.CompilerParams(dimension_semantics=semantics,
                                vmem_limit_bytes=VMEM_LIMIT_BYTES)


def _rmsnorm_kernel(x_ref, g_ref, o_ref):
    x = x_ref[...]
    ms = jnp.mean(x * x, axis=-1, keepdims=True)
    o_ref[...] = (x * lax.rsqrt(ms + EPS) * g_ref[...]).astype(o_ref.dtype)


def _rmsnorm(x, g, out_dtype, rows=512):
    m, d = x.shape
    return pl.pallas_call(
        _rmsnorm_kernel,
        grid=(m // rows,),
        in_specs=[pl.BlockSpec((rows, d), lambda i: (i, 0)),
                  pl.BlockSpec((1, d), lambda i: (0, 0))],
        out_specs=pl.BlockSpec((rows, d), lambda i: (i, 0)),
        out_shape=jax.ShapeDtypeStruct((m, d), out_dtype),
        compiler_params=_params(("parallel",)),
        name="rmsnorm",
    )(x, g.reshape(1, d).astype(F32))


def _matmul_kernel(*refs, n_lhs, nk, act, has_res, emit_prenorm, has_row_ssq, norm_dim):
    refs = list(refs)
    x_refs = [refs.pop(0) for _ in range(n_lhs)]
    w_ref = refs.pop(0)
    res_ref = refs.pop(0) if has_res else None
    gain_ref = refs.pop(0) if emit_prenorm else None
    ssq_in_ref = refs.pop(0) if has_row_ssq else None
    o_ref = refs.pop(0)
    if emit_prenorm:
        hg_ref, ssq_ref = refs

    def product():
        acc, k0 = None, 0
        for x_ref in x_refs:
            k1 = k0 + x_ref.shape[1]
            part = jnp.dot(x_ref[...], w_ref[k0:k1, :].astype(BF16),
                           preferred_element_type=F32)
            acc = part if acc is None else acc + part
            k0 = k1
        return acc

    def row_factor(width):
        r2 = 1.0 / (ssq_in_ref[...] / norm_dim + EPS)
        return jnp.concatenate([r2] * (width // LANES), axis=1)

    if nk == 1:
        acc = product()
        if act == "relu2":
            acc = jnp.square(jnp.maximum(acc, 0.0))
        if ssq_in_ref is not None:
            acc = acc * row_factor(acc.shape[1])
        if res_ref is not None:
            acc = res_ref[...] + acc
        o_ref[...] = acc.astype(o_ref.dtype)
        if emit_prenorm:
            hg_ref[...] = (acc * gain_ref[...]).astype(hg_ref.dtype)
            part = jnp.broadcast_to(jnp.sum(acc * acc, axis=1, keepdims=True),
                                    ssq_ref.shape)
            j = pl.program_id(1)

            @pl.when(j == 0)
            def _():
                ssq_ref[...] = part

            @pl.when(j > 0)
            def _():
                ssq_ref[...] += part
        return

    k = pl.program_id(2)

    @pl.when(k == 0)
    def _():
        o_ref[...] = product()

    @pl.when(jnp.logical_and(k > 0, k < nk - 1))
    def _():
        o_ref[...] += product()

    @pl.when(k == nk - 1)
    def _():
        acc = o_ref[...] + product()
        if ssq_in_ref is not None:
            acc = acc * row_factor(acc.shape[1])
        if res_ref is not None:
            acc = res_ref[...] + acc
        o_ref[...] = acc


def _matmul(xs, w, *, out_dtype, act=None, res=None, prenorm_gain=None, row_ssq=None,
            norm_dim=None, tm, tn, tk=None, lhs_buffers=None, name):
    m = xs[0].shape[0]
    kdim, n = w.shape
    assert sum(x.shape[1] for x in xs) == kdim
    tk = kdim if tk is None else tk
    nk = kdim // tk
    assert m % tm == 0 and n % tn == 0 and kdim % tk == 0
    assert nk == 1 or (len(xs) == 1 and out_dtype == F32 and act is None
                       and prenorm_gain is None)
    assert (row_ssq is None) == (norm_dim is None)
    mode = {} if lhs_buffers is None else dict(pipeline_mode=pl.Buffered(lhs_buffers))
    if nk == 1:
        in_specs = [pl.BlockSpec((tm, x.shape[1]), lambda i, j, k: (i, 0), **mode)
                    for x in xs]
    else:
        in_specs = [pl.BlockSpec((tm, tk), lambda i, j, k: (i, k), **mode)]
    in_specs.append(pl.BlockSpec((tk, tn), lambda i, j, k: (k, j)))
    args = [*xs, w]
    if res is not None:
        in_specs.append(pl.BlockSpec((tm, tn), lambda i, j, k: (i, j)))
        args.append(res)
    out_specs = pl.BlockSpec((tm, tn), lambda i, j, k: (i, j))
    out_shape = jax.ShapeDtypeStruct((m, n), out_dtype)
    row_spec = pl.BlockSpec((tm, LANES), lambda i, j, k: (i, 0))
    if prenorm_gain is not None:
        in_specs.append(pl.BlockSpec((1, tn), lambda i, j, k: (0, j)))
        args.append(prenorm_gain.reshape(1, n).astype(F32))
        out_specs = [out_specs, pl.BlockSpec((tm, tn), lambda i, j, k: (i, j)), row_spec]
        out_shape = [out_shape, jax.ShapeDtypeStruct((m, n), BF16),
                     jax.ShapeDtypeStruct((m, LANES), F32)]
    if row_ssq is not None:
        in_specs.append(row_spec)
        args.append(row_ssq)
    return pl.pallas_call(
        functools.partial(_matmul_kernel, n_lhs=len(xs), nk=nk, act=act,
                          has_res=res is not None,
                          emit_prenorm=prenorm_gain is not None,
                          has_row_ssq=row_ssq is not None, norm_dim=norm_dim),
        grid=(m // tm, n // tn, nk),
        in_specs=in_specs,
        out_specs=out_specs,
        out_shape=out_shape,
        compiler_params=_params(("parallel", "arbitrary", "arbitrary")),
        name=name,
    )(*args)


def _rider_specs(weights, n_steps, step_of):
    specs, shapes = [], []
    for w in weights:
        rows, cols = w.shape
        assert rows % n_steps == 0 and (rows // n_steps) % 16 == 0
        specs.append(pl.BlockSpec((rows // n_steps, cols),
                                  lambda *g: (step_of(*g), 0)))
        shapes.append(jax.ShapeDtypeStruct((rows, cols), BF16))
    return specs, shapes


def _cast_riders(in_refs, out_refs):
    for src, dst in zip(in_refs, out_refs):
        dst[...] = src[...].astype(dst.dtype)


def _retention_kernel(*refs, seq, n_riders):
    (cdec_ref, q_ref, k_ref, v_ref, g_ref, cos_ref, sin_ref,
     dmask_ref, qdec_ref, kdec_ref) = refs[:10]
    o_ref = refs[10 + n_riders]
    state_ref = refs[-1]
    _cast_riders(refs[10:10 + n_riders], refs[11 + n_riders:11 + 2 * n_riders])
    h = pl.program_id(1)
    c = RET_CHUNK
    half = RET_HEAD_DIM // 2
    cd = cdec_ref[h]
    k_scale = RET_HEAD_DIM ** -0.5
    state_ref[...] = jnp.zeros_like(state_ref)
    dmask = dmask_ref[...]
    qdec = qdec_ref[...]
    kdec = kdec_ref[...]

    def rot(t, cos, sin):
        t1, t2 = t[:, :half], t[:, half:]
        return jnp.concatenate([t1 * cos - t2 * sin, t1 * sin + t2 * cos], axis=-1)

    def chunk(n, carry):
        r0 = pl.multiple_of(n * c, c)
        rows = pl.ds(r0, c)
        cos = cos_ref[rows, :]
        sin = sin_ref[rows, :]
        qr = rot(q_ref[rows, :].astype(F32), cos, sin)
        kr = rot(k_ref[rows, :].astype(F32), cos, sin) * k_scale
        v = v_ref[rows, :]
        inner = lax.dot_general(qr.astype(BF16), kr.astype(BF16), NT_DIMS,
                                preferred_element_type=F32) * dmask
        y = jnp.dot(inner.astype(BF16), v, preferred_element_type=F32)
        state = state_ref[...]
        y = y + jnp.dot((qr * qdec).astype(BF16), state.astype(BF16),
                        preferred_element_type=F32)
        contrib = lax.dot_general((kr * kdec).astype(BF16), v, TN_DIMS,
                                  preferred_element_type=F32)
        state_ref[...] = state * cd + contrib
        yn = y * lax.rsqrt(jnp.mean(y * y, axis=-1, keepdims=True) + EPS)
        g = g_ref[rows, :].astype(F32)
        gate = g * (1.0 / (1.0 + jnp.exp(-g)))
        o_ref[rows, :] = (gate * yn).astype(o_ref.dtype)
        return carry

    lax.fori_loop(0, seq // c, chunk, 0, unroll=8)


def _retention(proj, batch, seq, n_heads, col0, riders):
    d = RET_HEAD_DIM
    c = RET_CHUNK
    half = d // 2
    blk0 = col0 // d

    pos = jnp.arange(seq)
    inv_freq = ROPE_BASE ** (-jnp.arange(half, dtype=F32) / half)
    ang = pos.astype(F32)[:, None] * inv_freq[None, :]
    cos = jnp.cos(ang)
    sin = jnp.sin(ang)
    log_gamma = jnp.log(1.0 - 2.0 ** (-5.0 - jnp.arange(n_heads, dtype=F32)))
    idx = jnp.arange(c, dtype=F32)
    diff = idx[:, None] - idx[None, :]
    dmask = jnp.where(diff >= 0,
                      jnp.exp(log_gamma[:, None, None] * jnp.maximum(diff, 0.0)),
                      0.0).astype(F32)
    kdec = jnp.exp(log_gamma[:, None] * (c - 1 - idx)).astype(F32)
    qdec = jnp.exp(log_gamma[:, None] * (idx + 1.0)).astype(F32)
    cdec = jnp.exp(log_gamma * c).astype(F32)
    kdec = jnp.broadcast_to(kdec[:, :, None], (n_heads, c, d))
    qdec = jnp.broadcast_to(qdec[:, :, None], (n_heads, c, d))

    def head_spec(group):
        return pl.BlockSpec((seq, d), lambda b, h: (b, blk0 + group * n_heads + h))

    rider_specs, rider_shapes = _rider_specs(riders, batch * n_heads,
                                             lambda b, h: b * n_heads + h)
    return pl.pallas_call(
        functools.partial(_retention_kernel, seq=seq, n_riders=len(riders)),
        grid=(batch, n_heads),
        in_specs=[pl.BlockSpec(memory_space=pltpu.SMEM),
                  head_spec(0), head_spec(1), head_spec(2), head_spec(3),
                  pl.BlockSpec((seq, half), lambda b, h: (0, 0)),
                  pl.BlockSpec((seq, half), lambda b, h: (0, 0)),
                  pl.BlockSpec((None, c, c), lambda b, h: (h, 0, 0)),
                  pl.BlockSpec((None, c, d), lambda b, h: (h, 0, 0)),
                  pl.BlockSpec((None, c, d), lambda b, h: (h, 0, 0)),
                  *rider_specs],
        out_specs=[pl.BlockSpec((seq, d), lambda b, h: (b, h)), *rider_specs],
        out_shape=[jax.ShapeDtypeStruct((batch * seq, n_heads * d), BF16), *rider_shapes],
        scratch_shapes=[pltpu.VMEM((d, d), F32)],
        compiler_params=_params(("parallel", "parallel")),
        name="retention",
    )(cdec, proj, proj, proj, proj, cos, sin, dmask, qdec, kdec, *riders)


def _t5_bucket(dist):
    n = jnp.maximum(dist, 0)
    max_exact = REL_BUCKETS // 2
    nf = jnp.maximum(n, 1).astype(F32)
    large = max_exact + (jnp.log(nf / max_exact) / math.log(REL_MAX_DIST / max_exact)
                         * (REL_BUCKETS - max_exact)).astype(jnp.int32)
    large = jnp.minimum(large, REL_BUCKETS - 1)
    return jnp.where(n < max_exact, n, large)


def _t5_bucket_static(n):
    n = np.maximum(n, 0)
    max_exact = REL_BUCKETS // 2
    nf = np.maximum(n, 1).astype(np.float32)
    large = max_exact + (np.log(nf / max_exact) / math.log(REL_MAX_DIST / max_exact)
                         * (REL_BUCKETS - max_exact)).astype(np.int32)
    large = np.minimum(large, REL_BUCKETS - 1)
    return np.where(n < max_exact, n, large)


def _reduce_rows(pair_op, final_op, x, rows=32):
    slabs = [x[i:i + rows, :] for i in range(0, x.shape[0], rows)]
    while len(slabs) > 1:
        nxt = [pair_op(slabs[i], slabs[i + 1]) for i in range(0, len(slabs) - 1, 2)]
        if len(slabs) % 2:
            nxt.append(slabs[-1])
        slabs = nxt
    return final_op(slabs[0], axis=0, keepdims=True)


def _moba_kernel(*refs, seq, n_riders):
    cfar_ref, q_ref, k_ref, v_ref, wrev_ref = refs[:5]
    o_ref = refs[5 + n_riders]
    t0_ref, t1_ref, qa_ref, ka_ref, vt_ref, s_ref, p_ref = refs[6 + 2 * n_riders:]
    _cast_riders(refs[5:5 + n_riders], refs[6 + n_riders:6 + 2 * n_riders])
    h = pl.program_id(0)
    b = pl.program_id(1)
    blk = MOBA_BLOCK
    d = MOBA_HEAD_DIM
    nb = seq // blk
    cfar = cfar_ref[h]

    @pl.when(b == 0)
    def _():
        w2 = jnp.broadcast_to(wrev_ref[...], (blk, 2 * blk))
        t0 = pltpu.roll(w2, 0, 1, stride=1, stride_axis=0)[:, :blk]
        t1 = pltpu.roll(w2, blk, 1, stride=1, stride_axis=0)[:, :blk]
        row = lax.broadcasted_iota(jnp.int32, (blk, blk), 0)
        col = lax.broadcasted_iota(jnp.int32, (blk, blk), 1)
        t0_ref[...] = jnp.where(row >= col, (t0 - cfar) * LOG2E, NEG_INF).T
        t1_ref[...] = ((t1 - cfar) * LOG2E).T

    qa_ref[:, :d] = (q_ref[...].astype(F32) * (d ** -0.5 * LOG2E)).astype(BF16)
    ka_ref[:, :d] = k_ref[...]
    ka_ref[:, d:] = (lax.broadcasted_iota(jnp.int32, (seq, LANES), 0) // blk ==
                     lax.broadcasted_iota(jnp.int32, (seq, LANES), 1)).astype(BF16)

    free = min((MOBA_TOPK + 1) * blk, seq)
    qa_ref[0:free, d:] = jnp.zeros((free, LANES), BF16)
    if free < seq:
        gated = seq - free
        kf = k_ref[...].astype(F32)
        kmean = jnp.concatenate(
            [jnp.mean(kf[j * blk:(j + 1) * blk, :], axis=0, keepdims=True)
             for j in range(nb)], axis=0)
        gate = lax.dot_general(kmean.astype(BF16), q_ref[free:seq, :], NT_DIMS,
                               preferred_element_type=F32)
        jidx = lax.broadcasted_iota(jnp.int32, (nb, gated), 0)
        qblk = (lax.broadcasted_iota(jnp.int32, (nb, gated), 1) + free) // blk
        rank = jnp.zeros((nb, gated), jnp.int32)
        for jp in range(nb):
            rowg = gate[jp:jp + 1, :]
            beats = jnp.logical_or(rowg > gate,
                                   jnp.logical_and(rowg == gate, jp < jidx))
            rank = rank + jnp.where(jnp.logical_and(beats, jp < qblk), 1, 0)
        keep = jnp.logical_or(jidx >= qblk, rank < MOBA_TOPK)
        negmask = jnp.where(keep, 0.0, NEG_INF).astype(BF16)
        eye = (lax.broadcasted_iota(jnp.int32, (nb, LANES), 0) ==
               lax.broadcasted_iota(jnp.int32, (nb, LANES), 1)).astype(BF16)
        mask_cols = lax.dot_general(negmask, eye, TN_DIMS,
                                    preferred_element_type=F32)
        qa_ref[free:seq, d:] = mask_cols.astype(BF16)

    vt_ref[...] = v_ref[...].astype(F32).T.astype(BF16)

    t0 = t0_ref[...]
    t1 = t1_ref[...]
    def scores(qb, slot):
        r0 = qb * blk
        w = r0 + blk
        s = lax.dot_general(ka_ref[0:w, :], qa_ref[r0:w, :], NT_DIMS,
                            preferred_element_type=F32)
        s_ref[slot, w - blk:w, :] = s[w - blk:, :] + t0
        if qb >= 1:
            s_ref[slot, w - 2 * blk:w - blk, :] = s[w - 2 * blk:w - blk, :] + t1
        if qb >= 2:
            s_ref[slot, 0:w - 2 * blk, :] = s[:w - 2 * blk, :]

    def softmax(qb, slot):
        w = (qb + 1) * blk
        s = s_ref[slot, 0:w, :]
        m = _reduce_rows(jnp.maximum, jnp.max, s)
        p = jnp.exp2(s - m)
        p_ref[slot, 0:w, :] = p.astype(BF16)
        return _reduce_rows(jnp.add, jnp.sum, p)

    def values(qb, slot, l):
        r0 = qb * blk
        w = r0 + blk
        o = jnp.dot(vt_ref[:, 0:w], p_ref[slot, 0:w, :],
                    preferred_element_type=F32)
        o_ref[r0:w, :] = (o / l).T.astype(o_ref.dtype)

    n_free = free // blk
    order = list(range(n_free)) + list(range(nb - 1, n_free - 1, -1))
    assert sorted(order) == list(range(nb))
    scores(order[0], 0)
    row_sum = softmax(order[0], 0)
    if nb > 1:
        scores(order[1], 1)
    for pos in range(nb):
        if pos + 2 < nb:
            scores(order[pos + 2], pos % 2)
        values(order[pos], pos % 2, row_sum)
        row_sum = softmax(order[pos + 1], (pos + 1) % 2) if pos + 1 < nb else None


def _moba(proj, rel_bias, batch, seq, n_heads, col0, riders):
    d = MOBA_HEAD_DIM
    blk = MOBA_BLOCK
    blk0 = col0 // d
    assert d == LANES and seq // blk <= LANES

    far = _t5_bucket_static(np.arange(blk + 1, max(seq, blk + 2)))
    assert (far == far[0]).all(), "far-block bias is not constant"
    bvec = rel_bias[_t5_bucket(jnp.arange(2 * blk))].T.astype(F32)
    cfar = rel_bias[_t5_bucket(jnp.arange(blk + 1, blk + 2))[0]].astype(F32)
    wrev = jnp.roll(bvec[:, ::-1], 1, axis=1).reshape(n_heads, 1, 2 * blk)

    def head_spec(group):
        return pl.BlockSpec((seq, d), lambda h, b: (b, blk0 + group * n_heads + h))

    rider_specs, rider_shapes = _rider_specs(riders, n_heads * batch,
                                             lambda h, b: h * batch + b)
    return pl.pallas_call(
        functools.partial(_moba_kernel, seq=seq, n_riders=len(riders)),
        grid=(n_heads, batch),
        in_specs=[pl.BlockSpec(memory_space=pltpu.SMEM),
                  head_spec(0), head_spec(1), head_spec(2),
                  pl.BlockSpec((None, 1, 2 * blk), lambda h, b: (h, 0, 0)),
                  *rider_specs],
        out_specs=[pl.BlockSpec((seq, d), lambda h, b: (b, h)), *rider_specs],
        out_shape=[jax.ShapeDtypeStruct((batch * seq, n_heads * d), BF16), *rider_shapes],
        scratch_shapes=[pltpu.VMEM((blk, blk), F32), pltpu.VMEM((blk, blk), F32),
                        pltpu.VMEM((seq, d + LANES), BF16),
                        pltpu.VMEM((seq, d + LANES), BF16),
                        pltpu.VMEM((d, seq), BF16),
                        pltpu.VMEM((2, seq, blk), F32),
                        pltpu.VMEM((2, seq, blk), BF16)],
        compiler_params=_params(("arbitrary", "arbitrary")),
        name="moba",
    )(cfar, proj, proj, proj, wrev, *riders)


def kernel(x, norm_mix, w_in, rel_bias, w_out, norm_ffn, w_up, w_down, norm_final):
    batch, seq, d_model = x.shape
    depth = w_in.shape[0]
    n_moba = rel_bias.shape[1]
    moba_width = n_moba * MOBA_HEAD_DIM
    mix_width = w_out.shape[1]
    ret_width = mix_width - moba_width
    n_ret = ret_width // RET_HEAD_DIM
    assert w_in.shape[2] == 4 * ret_width + 3 * moba_width
    assert seq % MOBA_BLOCK == 0 and seq % RET_CHUNK == 0

    xs = x.reshape(batch * seq, d_model)
    for l in range(depth):
        h = _rmsnorm(xs, norm_mix[l], BF16)
        proj = _matmul([h], w_in[l], out_dtype=BF16, name="in_proj", **F32_WEIGHT_TILES)
        (y_ret,) = _retention(proj, batch, seq, n_ret, 0, [])
        y_moba, w_down_bf, w_out_bf = _moba(proj, rel_bias, batch, seq, n_moba,
                                            4 * ret_width, [w_down[l], w_out[l]])
        xs, hg, ssq = _matmul([y_ret, y_moba], w_out_bf, out_dtype=F32, res=xs,
                              prenorm_gain=norm_ffn[l], name="out_proj",
                              **BF16_WEIGHT_TILES)
        a = _matmul([hg], w_up[l], out_dtype=BF16, act="relu2", name="ffn_up",
                    **F32_WEIGHT_TILES)
        xs = _matmul([a], w_down_bf, out_dtype=F32, res=xs, row_ssq=ssq,
                     norm_dim=d_model, name="ffn_down", **SPLIT_K_TILES)
    out = _rmsnorm(xs, norm_final, F32)
    return out.reshape(batch, seq, d_model)
```

```python
import functools
import math

import numpy as np
import jax
import jax.numpy as jnp
from jax import lax
from jax.experimental import pallas as pl
from jax.experimental.pallas import tpu as pltpu

F32 = jnp.float32
BF16 = jnp.bfloat16

RET_HEAD_DIM = 256
RET_CHUNK = 128
ROPE_BASE = 10000.0
MOBA_HEAD_DIM = 128
MOBA_BLOCK = 256
MOBA_TOPK = 3
REL_BUCKETS = 32
REL_MAX_DIST = 128
EPS = 1e-6
NEG_INF = -1e30
LOG2E = math.log2(math.e)

V7X_VMEM_BYTES = 64 * 1024 * 1024
VMEM_LIMIT_BYTES = V7X_VMEM_BYTES - 4 * 1024 * 1024
LANES = 128

F32_WEIGHT_TILES = dict(tm=2048, tn=512, lhs_buffers=1)
BF16_WEIGHT_TILES = dict(tm=1024, tn=512)
SPLIT_K_TILES = dict(tm=1024, tn=1024, tk=4096)

NT_DIMS = (((1,), (1,)), ((), ()))
TN_DIMS = (((0,), (0,)), ((), ()))


def _params(semantics):
    return pltpu.CompilerParams(dimension_semantics=semantics,
                                vmem_limit_bytes=VMEM_LIMIT_BYTES)


def _rmsnorm_kernel(x_ref, g_ref, o_ref):
    x = x_ref[...]
    ms = jnp.mean(x * x, axis=-1, keepdims=True)
    o_ref[...] = (x * lax.rsqrt(ms + EPS) * g_ref[...]).astype(o_ref.dtype)


def _rmsnorm(x, g, out_dtype, rows=512):
    m, d = x.shape
    return pl.pallas_call(
        _rmsnorm_kernel,
        grid=(m // rows,),
        in_specs=[pl.BlockSpec((rows, d), lambda i: (i, 0)),
                  pl.BlockSpec((1, d), lambda i: (0, 0))],
        out_specs=pl.BlockSpec((rows, d), lambda i: (i, 0)),
        out_shape=jax.ShapeDtypeStruct((m, d), out_dtype),
        compiler_params=_params(("parallel",)),
        name="rmsnorm",
    )(x, g.reshape(1, d).astype(F32))


def _matmul_kernel(*refs, n_lhs, nk, act, has_res, emit_prenorm, has_row_ssq, norm_dim):
    refs = list(refs)
    x_refs = [refs.pop(0) for _ in range(n_lhs)]
    w_ref = refs.pop(0)
    res_ref = refs.pop(0) if has_res else None
    gain_ref = refs.pop(0) if emit_prenorm else None
    ssq_in_ref = refs.pop(0) if has_row_ssq else None
    o_ref = refs.pop(0)
    if emit_prenorm:
        hg_ref, ssq_ref = refs

    def product():
        acc, k0 = None, 0
        for x_ref in x_refs:
            k1 = k0 + x_ref.shape[1]
            part = jnp.dot(x_ref[...], w_ref[k0:k1, :].astype(BF16),
                           preferred_element_type=F32)
            acc = part if acc is None else acc + part
            k0 = k1
        return acc

    def row_factor(width):
        r2 = 1.0 / (ssq_in_ref[...] / norm_dim + EPS)
        return jnp.concatenate([r2] * (width // LANES), axis=1)

    if nk == 1:
        acc = product()
        if act == "relu2":
            acc = jnp.square(jnp.maximum(acc, 0.0))
        if ssq_in_ref is not None:
            acc = acc * row_factor(acc.shape[1])
        if res_ref is not None:
            acc = res_ref[...] + acc
        o_ref[...] = acc.astype(o_ref.dtype)
        if emit_prenorm:
            hg_ref[...] = (acc * gain_ref[...]).astype(hg_ref.dtype)
            part = jnp.broadcast_to(jnp.sum(acc * acc, axis=1, keepdims=True),
                                    ssq_ref.shape)
            j = pl.program_id(1)

            @pl.when(j == 0)
            def _():
                ssq_ref[...] = part

            @pl.when(j > 0)
            def _():
                ssq_ref[...] += part
        return

    k = pl.program_id(2)

    @pl.when(k == 0)
    def _():
        o_ref[...] = product()

    @pl.when(jnp.logical_and(k > 0, k < nk - 1))
    def _():
        o_ref[...] += product()

    @pl.when(k == nk - 1)
    def _():
        acc = o_ref[...] + product()
        if ssq_in_ref is not None:
            acc = acc * row_factor(acc.shape[1])
        if res_ref is not None:
            acc = res_ref[...] + acc
        o_ref[...] = acc


def _matmul(xs, w, *, out_dtype, act=None, res=None, prenorm_gain=None, row_ssq=None,
            norm_dim=None, tm, tn, tk=None, lhs_buffers=None, name):
    m = xs[0].shape[0]
    kdim, n = w.shape
    assert sum(x.shape[1] for x in xs) == kdim
    tk = kdim if tk is None else tk
    nk = kdim // tk
    assert m % tm == 0 and n % tn == 0 and kdim % tk == 0
    assert nk == 1 or (len(xs) == 1 and out_dtype == F32 and act is None
                       and prenorm_gain is None)
    assert (row_ssq is None) == (norm_dim is None)
    mode = {} if lhs_buffers is None else dict(pipeline_mode=pl.Buffered(lhs_buffers))
    if nk == 1:
        in_specs = [pl.BlockSpec((tm, x.shape[1]), lambda i, j, k: (i, 0), **mode)
                    for x in xs]
    else:
        in_specs = [pl.BlockSpec((tm, tk), lambda i, j, k: (i, k), **mode)]
    in_specs.append(pl.BlockSpec((tk, tn), lambda i, j, k: (k, j)))
    args = [*xs, w]
    if res is not None:
        in_specs.append(pl.BlockSpec((tm, tn), lambda i, j, k: (i, j)))
        args.append(res)
    out_specs = pl.BlockSpec((tm, tn), lambda i, j, k: (i, j))
    out_shape = jax.ShapeDtypeStruct((m, n), out_dtype)
    row_spec = pl.BlockSpec((tm, LANES), lambda i, j, k: (i, 0))
    if prenorm_gain is not None:
        in_specs.append(pl.BlockSpec((1, tn), lambda i, j, k: (0, j)))
        args.append(prenorm_gain.reshape(1, n).astype(F32))
        out_specs = [out_specs, pl.BlockSpec((tm, tn), lambda i, j, k: (i, j)), row_spec]
        out_shape = [out_shape, jax.ShapeDtypeStruct((m, n), BF16),
                     jax.ShapeDtypeStruct((m, LANES), F32)]
    if row_ssq is not None:
        in_specs.append(row_spec)
        args.append(row_ssq)
    return pl.pallas_call(
        functools.partial(_matmul_kernel, n_lhs=len(xs), nk=nk, act=act,
                          has_res=res is not None,
                          emit_prenorm=prenorm_gain is not None,
                          has_row_ssq=row_ssq is not None, norm_dim=norm_dim),
        grid=(m // tm, n // tn, nk),
        in_specs=in_specs,
        out_specs=out_specs,
        out_shape=out_shape,
        compiler_params=_params(("parallel", "arbitrary", "arbitrary")),
        name=name,
    )(*args)


def _rider_specs(weights, n_steps, step_of):
    specs, shapes = [], []
    for w in weights:
        rows, cols = w.shape
        assert rows % n_steps == 0 and (rows // n_steps) % 16 == 0
        specs.append(pl.BlockSpec((rows // n_steps, cols),
                                  lambda *g: (step_of(*g), 0)))
        shapes.append(jax.ShapeDtypeStruct((rows, cols), BF16))
    return specs, shapes


def _cast_riders(in_refs, out_refs):
    for src, dst in zip(in_refs, out_refs):
        dst[...] = src[...].astype(dst.dtype)


def _retention_kernel(cdec_ref, q_ref, k_ref, v_ref, g_ref, cos_ref, sin_ref,
                      dmask_ref, qdec_ref, kdec_ref, o_ref, state_ref, *, seq):
    h = pl.program_id(1)
    c = RET_CHUNK
    half = RET_HEAD_DIM // 2
    cd = cdec_ref[h]
    k_scale = RET_HEAD_DIM ** -0.5
    state_ref[...] = jnp.zeros_like(state_ref)
    dmask = dmask_ref[...]
    qdec = qdec_ref[...]
    kdec = kdec_ref[...]

    def rot(t, cos, sin):
        t1, t2 = t[:, :half], t[:, half:]
        return jnp.concatenate([t1 * cos - t2 * sin, t1 * sin + t2 * cos], axis=-1)

    def chunk(n, carry):
        r0 = pl.multiple_of(n * c, c)
        rows = pl.ds(r0, c)
        cos = cos_ref[rows, :]
        sin = sin_ref[rows, :]
        qr = rot(q_ref[rows, :].astype(F32), cos, sin)
        kr = rot(k_ref[rows, :].astype(F32), cos, sin) * k_scale
        v = v_ref[rows, :]
        inner = lax.dot_general(qr.astype(BF16), kr.astype(BF16), NT_DIMS,
                                preferred_element_type=F32) * dmask
        y = jnp.dot(inner.astype(BF16), v, preferred_element_type=F32)
        state = state_ref[...]
        y = y + jnp.dot((qr * qdec).astype(BF16), state.astype(BF16),
                        preferred_element_type=F32)
        contrib = lax.dot_general((kr * kdec).astype(BF16), v, TN_DIMS,
                                  preferred_element_type=F32)
        state_ref[...] = state * cd + contrib
        yn = y * lax.rsqrt(jnp.mean(y * y, axis=-1, keepdims=True) + EPS)
        g = g_ref[rows, :].astype(F32)
        gate = g * (1.0 / (1.0 + jnp.exp(-g)))
        o_ref[rows, :] = (gate * yn).astype(o_ref.dtype)
        return carry

    lax.fori_loop(0, seq // c, chunk, 0, unroll=True)


def _retention(proj, batch, seq, n_heads, col0):
    d = RET_HEAD_DIM
    c = RET_CHUNK
    half = d // 2
    blk0 = col0 // d

    pos = jnp.arange(seq)
    inv_freq = ROPE_BASE ** (-jnp.arange(half, dtype=F32) / half)
    ang = pos.astype(F32)[:, None] * inv_freq[None, :]
    cos = jnp.cos(ang)
    sin = jnp.sin(ang)
    log_gamma = jnp.log(1.0 - 2.0 ** (-5.0 - jnp.arange(n_heads, dtype=F32)))
    idx = jnp.arange(c, dtype=F32)
    diff = idx[:, None] - idx[None, :]
    dmask = jnp.where(diff >= 0,
                      jnp.exp(log_gamma[:, None, None] * jnp.maximum(diff, 0.0)),
                      0.0).astype(F32)
    kdec = jnp.exp(log_gamma[:, None] * (c - 1 - idx)).astype(F32)
    qdec = jnp.exp(log_gamma[:, None] * (idx + 1.0)).astype(F32)
    cdec = jnp.exp(log_gamma * c).astype(F32)
    kdec = jnp.broadcast_to(kdec[:, :, None], (n_heads, c, d))
    qdec = jnp.broadcast_to(qdec[:, :, None], (n_heads, c, d))

    def head_spec(group):
        return pl.BlockSpec((seq, d), lambda b, h: (b, blk0 + group * n_heads + h))

    return pl.pallas_call(
        functools.partial(_retention_kernel, seq=seq),
        grid=(batch, n_heads),
        in_specs=[pl.BlockSpec(memory_space=pltpu.SMEM),
                  head_spec(0), head_spec(1), head_spec(2), head_spec(3),
                  pl.BlockSpec((seq, half), lambda b, h: (0, 0)),
                  pl.BlockSpec((seq, half), lambda b, h: (0, 0)),
                  pl.BlockSpec((None, c, c), lambda b, h: (h, 0, 0)),
                  pl.BlockSpec((None, c, d), lambda b, h: (h, 0, 0)),
                  pl.BlockSpec((None, c, d), lambda b, h: (h, 0, 0))],
        out_specs=pl.BlockSpec((seq, d), lambda b, h: (b, h)),
        out_shape=jax.ShapeDtypeStruct((batch * seq, n_heads * d), BF16),
        scratch_shapes=[pltpu.VMEM((d, d), F32)],
        compiler_params=_params(("parallel", "parallel")),
        name="retention",
    )(cdec, proj, proj, proj, proj, cos, sin, dmask, qdec, kdec)


def _t5_bucket(dist):
    n = jnp.maximum(dist, 0)
    max_exact = REL_BUCKETS // 2
    nf = jnp.maximum(n, 1).astype(F32)
    large = max_exact + (jnp.log(nf / max_exact) / math.log(REL_MAX_DIST / max_exact)
                         * (REL_BUCKETS - max_exact)).astype(jnp.int32)
    large = jnp.minimum(large, REL_BUCKETS - 1)
    return jnp.where(n < max_exact, n, large)


def _t5_bucket_static(n):
    n = np.maximum(n, 0)
    max_exact = REL_BUCKETS // 2
    nf = np.maximum(n, 1).astype(np.float32)
    large = max_exact + (np.log(nf / max_exact) / math.log(REL_MAX_DIST / max_exact)
                         * (REL_BUCKETS - max_exact)).astype(np.int32)
    large = np.minimum(large, REL_BUCKETS - 1)
    return np.where(n < max_exact, n, large)


def _reduce_rows(pair_op, final_op, x, rows=32):
    slabs = [x[i:i + rows, :] for i in range(0, x.shape[0], rows)]
    while len(slabs) > 1:
        nxt = [pair_op(slabs[i], slabs[i + 1]) for i in range(0, len(slabs) - 1, 2)]
        if len(slabs) % 2:
            nxt.append(slabs[-1])
        slabs = nxt
    return final_op(slabs[0], axis=0, keepdims=True)


def _moba_kernel(*refs, seq, n_riders):
    cfar_ref, q_ref, k_ref, v_ref, wrev_ref = refs[:5]
    o_ref = refs[5 + n_riders]
    t0_ref, t1_ref, qa_ref, ka_ref, vt_ref, s_ref, p_ref = refs[6 + 2 * n_riders:]
    _cast_riders(refs[5:5 + n_riders], refs[6 + n_riders:6 + 2 * n_riders])
    h = pl.program_id(0)
    b = pl.program_id(1)
    blk = MOBA_BLOCK
    d = MOBA_HEAD_DIM
    nb = seq // blk
    cfar = cfar_ref[h]

    @pl.when(b == 0)
    def _():
        w2 = jnp.broadcast_to(wrev_ref[...], (blk, 2 * blk))
        t0 = pltpu.roll(w2, 0, 1, stride=1, stride_axis=0)[:, :blk]
        t1 = pltpu.roll(w2, blk, 1, stride=1, stride_axis=0)[:, :blk]
        row = lax.broadcasted_iota(jnp.int32, (blk, blk), 0)
        col = lax.broadcasted_iota(jnp.int32, (blk, blk), 1)
        t0_ref[...] = jnp.where(row >= col, (t0 - cfar) * LOG2E, NEG_INF).T
        t1_ref[...] = ((t1 - cfar) * LOG2E).T

    qa_ref[:, :d] = (q_ref[...].astype(F32) * (d ** -0.5 * LOG2E)).astype(BF16)
    ka_ref[:, :d] = k_ref[...]
    ka_ref[:, d:] = (lax.broadcasted_iota(jnp.int32, (seq, LANES), 0) // blk ==
                     lax.broadcasted_iota(jnp.int32, (seq, LANES), 1)).astype(BF16)

    free = min((MOBA_TOPK + 1) * blk, seq)
    qa_ref[0:free, d:] = jnp.zeros((free, LANES), BF16)
    if free < seq:
        gated = seq - free
        kf = k_ref[...].astype(F32)
        kmean = jnp.concatenate(
            [jnp.mean(kf[j * blk:(j + 1) * blk, :], axis=0, keepdims=True)
             for j in range(nb)], axis=0)
        gate = lax.dot_general(kmean.astype(BF16), q_ref[free:seq, :], NT_DIMS,
                               preferred_element_type=F32)
        jidx = lax.broadcasted_iota(jnp.int32, (nb, gated), 0)
        qblk = (lax.broadcasted_iota(jnp.int32, (nb, gated), 1) + free) // blk
        rank = jnp.zeros((nb, gated), jnp.int32)
        for jp in range(nb):
            rowg = gate[jp:jp + 1, :]
            beats = jnp.logical_or(rowg > gate,
                                   jnp.logical_and(rowg == gate, jp < jidx))
            rank = rank + jnp.where(jnp.logical_and(beats, jp < qblk), 1, 0)
        keep = jnp.logical_or(jidx >= qblk, rank < MOBA_TOPK)
        negmask = jnp.where(keep, 0.0, NEG_INF).astype(BF16)
        eye = (lax.broadcasted_iota(jnp.int32, (nb, LANES), 0) ==
               lax.broadcasted_iota(jnp.int32, (nb, LANES), 1)).astype(BF16)
        mask_cols = lax.dot_general(negmask, eye, TN_DIMS,
                                    preferred_element_type=F32)
        qa_ref[free:seq, d:] = mask_cols.astype(BF16)

    vt_ref[...] = v_ref[...].astype(F32).T.astype(BF16)

    t0 = t0_ref[...]
    t1 = t1_ref[...]
    def scores(qb, slot):
        r0 = qb * blk
        w = r0 + blk
        s = lax.dot_general(ka_ref[0:w, :], qa_ref[r0:w, :], NT_DIMS,
                            preferred_element_type=F32)
        s_ref[slot, w - blk:w, :] = s[w - blk:, :] + t0
        if qb >= 1:
            s_ref[slot, w - 2 * blk:w - blk, :] = s[w - 2 * blk:w - blk, :] + t1
        if qb >= 2:
            s_ref[slot, 0:w - 2 * blk, :] = s[:w - 2 * blk, :]

    def softmax(qb, slot):
        w = (qb + 1) * blk
        s = s_ref[slot, 0:w, :]
        m = _reduce_rows(jnp.maximum, jnp.max, s)
        p = jnp.exp2(s - m)
        p_ref[slot, 0:w, :] = p.astype(BF16)
        return _reduce_rows(jnp.add, jnp.sum, p)

    def values(qb, slot, l):
        r0 = qb * blk
        w = r0 + blk
        o = jnp.dot(vt_ref[:, 0:w], p_ref[slot, 0:w, :],
                    preferred_element_type=F32)
        o_ref[r0:w, :] = (o / l).T.astype(o_ref.dtype)

    n_free = free // blk
    order = list(range(n_free)) + list(range(nb - 1, n_free - 1, -1))
    assert sorted(order) == list(range(nb))
    scores(order[0], 0)
    row_sum = softmax(order[0], 0)
    if nb > 1:
        scores(order[1], 1)
    for pos in range(nb):
        if pos + 2 < nb:
            scores(order[pos + 2], pos % 2)
        values(order[pos], pos % 2, row_sum)
        row_sum = softmax(order[pos + 1], (pos + 1) % 2) if pos + 1 < nb else None


def _moba(proj, rel_bias, batch, seq, n_heads, col0, riders):
    d = MOBA_HEAD_DIM
    blk = MOBA_BLOCK
    blk0 = col0 // d
    assert d == LANES and seq // blk <= LANES

    far = _t5_bucket_static(np.arange(blk + 1, max(seq, blk + 2)))
    assert (far == far[0]).all(), "far-block bias is not constant"
    bvec = rel_bias[_t5_bucket(jnp.arange(2 * blk))].T.astype(F32)
    cfar = rel_bias[_t5_bucket(jnp.arange(blk + 1, blk + 2))[0]].astype(F32)
    wrev = jnp.roll(bvec[:, ::-1], 1, axis=1).reshape(n_heads, 1, 2 * blk)

    def head_spec(group):
        return pl.BlockSpec((seq, d), lambda h, b: (b, blk0 + group * n_heads + h))

    rider_specs, rider_shapes = _rider_specs(riders, n_heads * batch,
                                             lambda h, b: h * batch + b)
    return pl.pallas_call(
        functools.partial(_moba_kernel, seq=seq, n_riders=len(riders)),
        grid=(n_heads, batch),
        in_specs=[pl.BlockSpec(memory_space=pltpu.SMEM),
                  head_spec(0), head_spec(1), head_spec(2),
                  pl.BlockSpec((None, 1, 2 * blk), lambda h, b: (h, 0, 0)),
                  *rider_specs],
        out_specs=[pl.BlockSpec((seq, d), lambda h, b: (b, h)), *rider_specs],
        out_shape=[jax.ShapeDtypeStruct((batch * seq, n_heads * d), BF16), *rider_shapes],
        scratch_shapes=[pltpu.VMEM((blk, blk), F32), pltpu.VMEM((blk, blk), F32),
                        pltpu.VMEM((seq, d + LANES), BF16),
                        pltpu.VMEM((seq, d + LANES), BF16),
                        pltpu.VMEM((d, seq), BF16),
                        pltpu.VMEM((2, seq, blk), F32),
                        pltpu.VMEM((2, seq, blk), BF16)],
        compiler_params=_params(("arbitrary", "arbitrary")),
        name="moba",
    )(cfar, proj, proj, proj, wrev, *riders)


def kernel(x, norm_mix, w_in, rel_bias, w_out, norm_ffn, w_up, w_down, norm_final):
    batch, seq, d_model = x.shape
    depth = w_in.shape[0]
    n_moba = rel_bias.shape[1]
    moba_width = n_moba * MOBA_HEAD_DIM
    mix_width = w_out.shape[1]
    ret_width = mix_width - moba_width
    n_ret = ret_width // RET_HEAD_DIM
    assert w_in.shape[2] == 4 * ret_width + 3 * moba_width
    assert seq % MOBA_BLOCK == 0 and seq % RET_CHUNK == 0

    xs = x.reshape(batch * seq, d_model)
    for l in range(depth):
        h = _rmsnorm(xs, norm_mix[l], BF16)
        proj = _matmul([h], w_in[l], out_dtype=BF16, name="in_proj", **F32_WEIGHT_TILES)
        y_ret = _retention(proj, batch, seq, n_ret, 0)
        y_moba, w_down_bf, w_out_bf = _moba(proj, rel_bias, batch, seq, n_moba,
                                            4 * ret_width, [w_down[l], w_out[l]])
        xs, hg, ssq = _matmul([y_ret, y_moba], w_out_bf, out_dtype=F32, res=xs,
                              prenorm_gain=norm_ffn[l], name="out_proj",
                              **BF16_WEIGHT_TILES)
        a = _matmul([hg], w_up[l], out_dtype=BF16, act="relu2", name="ffn_up",
                    **F32_WEIGHT_TILES)
        xs = _matmul([a], w_down_bf, out_dtype=F32, res=xs, row_ssq=ssq,
                     norm_dim=d_model, name="ffn_down", **SPLIT_K_TILES)
    out = _rmsnorm(xs, norm_final, F32)
    return out.reshape(batch, seq, d_model)
```
